```python
import jax, jax.numpy as jnp
from jax import lax
import numpy as np

D_MODEL = 1024
BATCH = 1
SEQ = 16384
DEPTH = 1
DEC_BATCH = 16
DEC_SEQ = 64
PAST_LEN = 2048

CHUNK = 64
D_RNN = D_MODEL
N_LRU_BLOCKS = 16
LRU_BLOCK = D_RNN // N_LRU_BLOCKS
CONV_WIDTH = 4
LRU_C = 8.0
N_HEADS = 16
HEAD_DIM = 64
D_ATT = N_HEADS * HEAD_DIM
Q_BLOCK = 128
D_FF = ((8 * D_MODEL // 3 + 255) // 256) * 256
EPS = 1e-6
D_IN = 2 * D_RNN + 3 * D_ATT + 2 * D_MODEL + N_HEADS
SPLIT_POINTS = (D_RNN, 2 * D_RNN, 2 * D_RNN + D_ATT, 2 * D_RNN + 2 * D_ATT, 2 * D_RNN + 3 * D_ATT, 2 * D_RNN + 3 * D_ATT + 2 * D_MODEL)

kernel_name = "hybrid_rglru_fox_stream_step"


def rmsnorm(x, g):
    xf = x.astype(jnp.float32)
    y = xf * lax.rsqrt(jnp.mean(xf * xf, axis=-1, keepdims=True) + EPS)
    return (y * g.astype(jnp.float32)).astype(x.dtype)


def causal_dwconv(x, buf, w, b):
    t = x.shape[1]
    xp = jnp.concatenate([buf.astype(x.dtype), x], axis=1)
    out = b
    for j in range(CONV_WIDTH):
        out = out + xp[:, j:j + t] * w[j]
    return out.astype(x.dtype), xp[:, -(CONV_WIDTH - 1):]


def rg_lru(x, h0, w_rg, b_rg, w_ig, b_ig, lam, reset_first):
    f32 = jnp.float32
    bsz, t, _ = x.shape
    xb = x.reshape(bsz, t, N_LRU_BLOCKS, LRU_BLOCK)
    r = jax.nn.sigmoid((jnp.einsum('bthi,hij->bthj', xb, w_rg).reshape(bsz, t, D_RNN) + b_rg).astype(f32))
    i = jax.nn.sigmoid((jnp.einsum('bthi,hij->bthj', xb, w_ig).reshape(bsz, t, D_RNN) + b_ig).astype(f32))
    log_a = LRU_C * r * jax.nn.log_sigmoid(lam.astype(f32))
    a = jnp.exp(log_a)
    mult = jnp.sqrt(-jnp.expm1(2.0 * log_a))
    if reset_first:
        mult = mult.at[:, 0].set(1.0)
    bterm = mult * i * x.astype(f32)
    bterm = bterm.at[:, 0].add(a[:, 0] * h0.astype(f32))

    def combine(left, right):
        a_l, b_l = left
        a_r, b_r = right
        return a_l * a_r, a_r * b_l + b_r

    _, h = lax.associative_scan(combine, (a, bterm), axis=1)
    return h.astype(x.dtype), h[:, -1]


def fox_attend(qb, cq, q_pos, k, v, ck, k_pos):
    s = jnp.einsum('bqhd,bkhd->bhqk', qb, k).astype(jnp.float32) * (HEAD_DIM ** -0.5)
    bias = jnp.transpose(cq, (0, 2, 1))[:, :, :, None] - jnp.transpose(ck, (0, 2, 1))[:, :, None, :]
    allowed = k_pos[None, None, None, :] <= q_pos[None, None, :, None]
    s = jnp.where(allowed, s + bias, -jnp.inf)
    p = jax.nn.softmax(s, axis=-1)
    return jnp.einsum('bhqk,bkhd->bqhd', p.astype(v.dtype), v)


def layer(x, conv_buf, h0, past_k, past_v, past_logf, norm_mix, w_in, b_forget, b_gate, conv_w, conv_b,
          w_rg, b_rg, w_ig, b_ig, lru_lambda, w_br_lru, w_br_att, w_out, norm_ffn, w_ffn_in, w_ffn_out):
    f32 = jnp.float32
    bsz, t, _ = x.shape
    xn = rmsnorm(x, norm_mix)
    proj = xn @ w_in
    x_rnn, gate_rnn, q, k, v, g_logit, f_logit = jnp.split(proj, SPLIT_POINTS, axis=-1)

    xc, new_buf = causal_dwconv(x_rnn, conv_buf, conv_w, conv_b)
    h_seq, h_last = rg_lru(xc, h0, w_rg, b_rg, w_ig, b_ig, lru_lambda, past_k is None)
    y_lru = h_seq * jax.nn.gelu(gate_rnn)

    q = q.reshape(bsz, t, N_HEADS, HEAD_DIM)
    k = k.reshape(bsz, t, N_HEADS, HEAD_DIM)
    v = v.reshape(bsz, t, N_HEADS, HEAD_DIM)
    logf = jax.nn.log_sigmoid((f_logit + b_forget).astype(f32))
    if past_k is None:
        c = jnp.cumsum(logf, axis=1)
        pos = jnp.arange(t, dtype=jnp.int32)
        nb = t // Q_BLOCK
        qs = jnp.moveaxis(q.reshape(bsz, nb, Q_BLOCK, N_HEADS, HEAD_DIM), 1, 0)
        cs = jnp.moveaxis(c.reshape(bsz, nb, Q_BLOCK, N_HEADS), 1, 0)
        ps = pos.reshape(nb, Q_BLOCK)

        def blk(args):
            qb, cqb, pb = args
            return fox_attend(qb, cqb, pb, k, v, c, pos)

        o = lax.map(blk, (qs, cs, ps))
        y_att = jnp.moveaxis(o, 0, 1).reshape(bsz, t, D_ATT)
    else:
        p_len = past_k.shape[1]
        k_all = jnp.concatenate([past_k.astype(k.dtype), k], axis=1)
        v_all = jnp.concatenate([past_v.astype(v.dtype), v], axis=1)
        c_all = jnp.cumsum(jnp.concatenate([past_logf.astype(f32), logf], axis=1), axis=1)
        q_pos = p_len + jnp.arange(t, dtype=jnp.int32)
        k_pos = jnp.arange(p_len + t, dtype=jnp.int32)
        y_att = fox_attend(q, c_all[:, p_len:], q_pos, k_all, v_all, c_all, k_pos).reshape(bsz, t, D_ATT)

    g = jax.nn.sigmoid((g_logit + b_gate).astype(f32))
    g_lru, g_att = jnp.split(g, [D_MODEL], axis=-1)
    mixed = g_lru * (y_lru @ w_br_lru).astype(f32) + g_att * (y_att @ w_br_att).astype(f32)
    x = x + mixed.astype(x.dtype) @ w_out

    xn2 = rmsnorm(x, norm_ffn)
    gf, up = jnp.split(xn2 @ w_ffn_in, [D_FF], axis=-1)
    x = x + (jax.nn.silu(gf) * up) @ w_ffn_out
    return x, k, v, logf.astype(x.dtype), new_buf, h_last.astype(x.dtype)


def setup_inputs(seed: int = 0) -> dict:
    key = jax.random.key(seed)
    ks = jax.random.split(key, 28)
    nrm = jax.random.normal
    L = DEPTH
    u = jax.random.uniform(ks[13], (L, D_RNN), minval=0.9, maxval=0.999)
    a_base = u ** (1.0 / LRU_C)
    lru_lambda = jnp.log(a_base) - jnp.log1p(-a_base)
    return {
        "x_prompt": nrm(ks[0], (BATCH, SEQ, D_MODEL), jnp.float32),
        "x_sample": nrm(ks[1], (DEC_BATCH, DEC_SEQ, D_MODEL), jnp.float32),
        "cache_k": nrm(ks[2], (L, DEC_BATCH, PAST_LEN, N_HEADS, HEAD_DIM), jnp.float32),
        "cache_v": nrm(ks[3], (L, DEC_BATCH, PAST_LEN, N_HEADS, HEAD_DIM), jnp.float32),
        "cache_logf": jax.nn.log_sigmoid(3.0 + nrm(ks[4], (L, DEC_BATCH, PAST_LEN, N_HEADS), jnp.float32)),
        "state_conv": nrm(ks[5], (L, DEC_BATCH, CONV_WIDTH - 1, D_RNN), jnp.float32),
        "state_h": 0.5 * nrm(ks[6], (L, DEC_BATCH, D_RNN), jnp.float32),
        "norm_mix": 1.0 + 0.05 * nrm(ks[7], (L, D_MODEL), jnp.float32),
        "w_in": nrm(ks[8], (L, D_MODEL, D_IN), jnp.float32) * D_MODEL ** -0.5,
        "b_forget": jax.random.uniform(ks[9], (L, N_HEADS), minval=1.0, maxval=5.0),
        "b_gate": 0.1 * nrm(ks[10], (L, 2 * D_MODEL), jnp.float32),
        "conv_w": 0.5 * nrm(ks[11], (L, CONV_WIDTH, D_RNN), jnp.float32),
        "conv_b": 0.05 * nrm(ks[12], (L, D_RNN), jnp.float32),
        "w_rg": nrm(ks[14], (L, N_LRU_BLOCKS, LRU_BLOCK, LRU_BLOCK), jnp.float32) * LRU_BLOCK ** -0.5,
        "b_rg": 0.05 * nrm(ks[15], (L, D_RNN), jnp.float32),
        "w_ig": nrm(ks[16], (L, N_LRU_BLOCKS, LRU_BLOCK, LRU_BLOCK), jnp.float32) * LRU_BLOCK ** -0.5,
        "b_ig": 0.05 * nrm(ks[17], (L, D_RNN), jnp.float32),
        "lru_lambda": lru_lambda,
        "w_br_lru": nrm(ks[18], (L, D_RNN, D_MODEL), jnp.float32) * D_RNN ** -0.5,
        "w_br_att": nrm(ks[19], (L, D_ATT, D_MODEL), jnp.float32) * D_ATT ** -0.5,
        "w_out": nrm(ks[20], (L, D_MODEL, D_MODEL), jnp.float32) * D_MODEL ** -0.5,
        "norm_ffn": 1.0 + 0.05 * nrm(ks[21], (L, D_MODEL), jnp.float32),
        "w_ffn_in": nrm(ks[22], (L, D_MODEL, 2 * D_FF), jnp.float32) * D_MODEL ** -0.5,
        "w_ffn_out": nrm(ks[23], (L, D_FF, D_MODEL), jnp.float32) * D_FF ** -0.5,
        "norm_final": 1.0 + 0.05 * nrm(ks[24], (D_MODEL,), jnp.float32),
    }


def reference(x_prompt, x_sample, cache_k, cache_v, cache_logf, state_conv, state_h,
              norm_mix, w_in, b_forget, b_gate, conv_w, conv_b, w_rg, b_rg, w_ig, b_ig, lru_lambda,
              w_br_lru, w_br_att, w_out, norm_ffn, w_ffn_in, w_ffn_out, norm_final):
    bp = x_prompt.shape[0]
    xp = x_prompt
    xs = x_sample
    kp_l, vp_l, lfp_l, cp_l, hp_l = [], [], [], [], []
    ks_l, vs_l, lfs_l, cs_l, hs_l = [], [], [], [], []
    for l in range(DEPTH):
        w = (norm_mix[l], w_in[l], b_forget[l], b_gate[l], conv_w[l], conv_b[l], w_rg[l], b_rg[l],
             w_ig[l], b_ig[l], lru_lambda[l], w_br_lru[l], w_br_att[l], w_out[l], norm_ffn[l],
             w_ffn_in[l], w_ffn_out[l])
        zero_buf = jnp.zeros((bp, CONV_WIDTH - 1, D_RNN), xp.dtype)
        zero_h = jnp.zeros((bp, D_RNN), xp.dtype)
        xp, kp, vp, lfp, cp, hp = layer(xp, zero_buf, zero_h, None, None, None, *w)
        xs, ksm, vsm, lfs, csm, hsm = layer(xs, state_conv[l], state_h[l], cache_k[l], cache_v[l], cache_logf[l], *w)
        kp_l.append(kp); vp_l.append(vp); lfp_l.append(lfp); cp_l.append(cp); hp_l.append(hp)
        ks_l.append(ksm); vs_l.append(vsm); lfs_l.append(lfs); cs_l.append(csm); hs_l.append(hsm)
    y_prompt = rmsnorm(xp, norm_final)
    y_sample = rmsnorm(xs, norm_final)
    return (y_prompt, y_sample,
            jnp.stack(kp_l), jnp.stack(vp_l), jnp.stack(lfp_l), jnp.stack(cp_l), jnp.stack(hp_l),
            jnp.stack(ks_l), jnp.stack(vs_l), jnp.stack(lfs_l), jnp.stack(cs_l), jnp.stack(hs_l))
```

```python
import functools
import math

import jax
import jax.numpy as jnp
from jax import lax
from jax.experimental import pallas as pl
from jax.experimental.pallas import tpu as pltpu

N_HEADS = 16
HEAD_DIM = 64
N_LRU_BLOCKS = 16
CONV_WIDTH = 4
LRU_C = 8.0
EPS = 1e-6

LANES = 128
MXU_DIM = 256
VMEM_LIMIT_BYTES = 56 * 1024 * 1024
CARRY_ROWS = 8

F32 = jnp.float32
BF16 = jnp.bfloat16


def _log_sigmoid(x):
    return jnp.minimum(x, 0.0) - jnp.log1p(jnp.exp(-jnp.abs(x)))


def _gelu_tanh(x):
    c = math.sqrt(2.0 / math.pi)
    return 0.5 * x * (1.0 + jnp.tanh(c * (x + 0.044715 * (x * x * x))))


def _rmsnorm(x, g):
    return x * lax.rsqrt(jnp.mean(x * x, axis=-1, keepdims=True) + EPS) * g


def _scan_affine_rows(a, b):
    n = a.shape[0]
    row = lax.broadcasted_iota(jnp.int32, a.shape, 0)
    d = 1
    while d < n:
        keep = row >= d
        a_sh = jnp.where(keep, pltpu.roll(a, d, 0), 1.0)
        b_sh = jnp.where(keep, pltpu.roll(b, d, 0), 0.0)
        b = a * b_sh + b
        a = a * a_sh
        d *= 2
    return a, b


def _cumsum_rows(x):
    n = x.shape[0]
    row = lax.broadcasted_iota(jnp.int32, x.shape, 0)
    d = 1
    while d < n:
        x = x + jnp.where(row >= d, pltpu.roll(x, d, 0), 0.0)
        d *= 2
    return x


def _const_spec(shape):
    return pl.BlockSpec(shape, lambda *_: (0,) * len(shape), pipeline_mode=pl.Buffered(1))


def _proj_lru_kernel(x_ref, conv0_ref, h0_ref, nmix_ref, win_ref, convw_ref, convb_ref, wgate_ref,
                     brg_ref, big_ref, lam_ref, bf_ref,
                     q_ref, kb_ref, vb_ref, vt_ref, kf_ref, vf_ref, logf_ref, ylru_ref, convo_ref, hlast_ref,
                     xr_buf, h_carry, *, reset_first):
    t = pl.program_id(1)
    tb, d = x_ref.shape[1], x_ref.shape[2]
    a_dim = q_ref.shape[2]

    xb = _rmsnorm(x_ref[0], nmix_ref[...]).astype(BF16)

    def proj(lo, width):
        return jnp.dot(xb, win_ref[:, lo:lo + width], preferred_element_type=F32)

    xr = proj(0, d)
    gate = proj(d, d)
    q = proj(2 * d, a_dim)
    k = proj(2 * d + a_dim, a_dim)
    v = proj(2 * d + 2 * a_dim, a_dim)
    f_logit = proj(2 * d + 3 * a_dim, LANES)

    q_ref[0] = (q * (HEAD_DIM ** -0.5)).astype(BF16)
    kb_ref[0] = k.astype(BF16)
    vb_ref[0] = v.astype(BF16)
    vt_ref[0] = v.T.astype(BF16)
    kf_ref[0] = k
    vf_ref[0] = v
    logf_ref[0] = _log_sigmoid(f_logit + bf_ref[...])

    lo = CARRY_ROWS - (CONV_WIDTH - 1)

    @pl.when(t == 0)
    def _():
        xr_buf[lo:CARRY_ROWS, :] = conv0_ref[0]
        h_carry[...] = h0_ref[0]

    xr_buf[CARRY_ROWS:CARRY_ROWS + tb, :] = xr
    xc = convb_ref[...] + xr_buf[lo:lo + tb, :] * convw_ref[0:1, :]
    for j in range(1, CONV_WIDTH):
        xc = xc + xr_buf[lo + j:lo + j + tb, :] * convw_ref[j:j + 1, :]
    tail = xr_buf[tb + lo:tb + CARRY_ROWS, :]
    convo_ref[0] = tail
    xr_buf[lo:CARRY_ROWS, :] = tail

    xcb = xc.astype(BF16)
    parts = [jnp.dot(xcb[:, g * MXU_DIM:(g + 1) * MXU_DIM], wgate_ref[g], preferred_element_type=F32)
             for g in range(d // MXU_DIM)]
    r = jax.nn.sigmoid(jnp.concatenate([p[:, :MXU_DIM] for p in parts], axis=1) + brg_ref[...])
    i = jax.nn.sigmoid(jnp.concatenate([p[:, MXU_DIM:] for p in parts], axis=1) + big_ref[...])
    a = jnp.exp(LRU_C * r * _log_sigmoid(lam_ref[...]))
    mult = jnp.sqrt(1.0 - a * a)
    if reset_first:
        row = lax.broadcasted_iota(jnp.int32, mult.shape, 0)
        mult = jnp.where(jnp.logical_and(row == 0, t == 0), 1.0, mult)
    a_cum, b_cum = _scan_affine_rows(a, mult * i * xc)
    h = a_cum * h_carry[...] + b_cum
    h_carry[...] = h[tb - 1:tb, :]
    hlast_ref[0] = h[tb - 1:tb, :]
    ylru_ref[0] = (h * _gelu_tanh(gate)).astype(BF16)


def _proj_lru(x, conv0, h0, w, *, reset_first, tb):
    b, t, d = x.shape
    a_dim = N_HEADS * HEAD_DIM
    tb = min(tb, t)
    grid = (b, t // tb)
    tok = lambda width: pl.BlockSpec((1, tb, width), lambda bi, ti: (bi, ti, 0))
    per_b = lambda rows: pl.BlockSpec((1, rows, d), lambda bi, ti: (bi, 0, 0))
    out_shapes = (
        jax.ShapeDtypeStruct((b, t, a_dim), BF16),
        jax.ShapeDtypeStruct((b, t, a_dim), BF16),
        jax.ShapeDtypeStruct((b, t, a_dim), BF16),
        jax.ShapeDtypeStruct((b, a_dim, t), BF16),
        jax.ShapeDtypeStruct((b, t, a_dim), F32),
        jax.ShapeDtypeStruct((b, t, a_dim), F32),
        jax.ShapeDtypeStruct((b, t, LANES), F32),
        jax.ShapeDtypeStruct((b, t, d), BF16),
        jax.ShapeDtypeStruct((b, CONV_WIDTH - 1, d), F32),
        jax.ShapeDtypeStruct((b, 1, d), F32),
    )
    out_specs = (
        tok(a_dim), tok(a_dim), tok(a_dim),
        pl.BlockSpec((1, a_dim, tb), lambda bi, ti: (bi, 0, ti)),
        tok(a_dim), tok(a_dim), tok(LANES), tok(d),
        per_b(CONV_WIDTH - 1), per_b(1),
    )
    in_specs = [
        tok(d), per_b(CONV_WIDTH - 1), per_b(1),
        _const_spec(w["norm_mix"].shape), _const_spec(w["w_in1"].shape),
        _const_spec(w["conv_w"].shape), _const_spec(w["conv_b"].shape), _const_spec(w["w_gates"].shape),
        _const_spec(w["b_rg"].shape), _const_spec(w["b_ig"].shape), _const_spec(w["lam"].shape),
        _const_spec(w["b_forget"].shape),
    ]
    return pl.pallas_call(
        functools.partial(_proj_lru_kernel, reset_first=reset_first),
        grid=grid, in_specs=in_specs, out_specs=out_specs, out_shape=out_shapes,
        scratch_shapes=[pltpu.VMEM((CARRY_ROWS + tb, d), F32), pltpu.VMEM((1, d), F32)],
        compiler_params=pltpu.CompilerParams(
            dimension_semantics=("arbitrary", "arbitrary"), vmem_limit_bytes=VMEM_LIMIT_BYTES),
        name="proj_lru",
    )(x, conv0, h0, w["norm_mix"], w["w_in1"], w["conv_w"], w["conv_b"], w["w_gates"],
      w["b_rg"], w["b_ig"], w["lam"], w["b_forget"])


def _cumsum_kernel(lf_ref, off_ref, c_ref, ct_ref, carry):
    @pl.when(pl.program_id(1) == 0)
    def _():
        carry[...] = off_ref[0]

    n = lf_ref.shape[1]
    c = _cumsum_rows(lf_ref[0]) + carry[...]
    carry[...] = c[n - 1:n, :]
    c_ref[0] = c
    ct_ref[0] = c.T


def _cumsum_heads(logf, offset, *, tc):
    b, t, _ = logf.shape
    tc = min(tc, t)
    return pl.pallas_call(
        _cumsum_kernel,
        grid=(b, t // tc),
        in_specs=[pl.BlockSpec((1, tc, LANES), lambda bi, ti: (bi, ti, 0)),
                  pl.BlockSpec((1, 1, LANES), lambda bi, ti: (bi, 0, 0))],
        out_specs=(pl.BlockSpec((1, tc, LANES), lambda bi, ti: (bi, ti, 0)),
                   pl.BlockSpec((1, LANES, tc), lambda bi, ti: (bi, 0, ti))),
        out_shape=(jax.ShapeDtypeStruct((b, t, LANES), F32), jax.ShapeDtypeStruct((b, LANES, t), F32)),
        scratch_shapes=[pltpu.VMEM((1, LANES), F32)],
        compiler_params=pltpu.CompilerParams(dimension_semantics=("arbitrary", "arbitrary")),
        name="cumsum_heads",
    )(logf, offset)


def _prompt_attn_kernel(q_ref, k_ref, vt_ref, c_ref, ct_ref, o_ref, qm_ref, m_ref, l_ref, acc_ref):
    j, i = pl.program_id(1), pl.program_id(2)
    tq, tk = q_ref.shape[1], k_ref.shape[1]
    lane = lax.broadcasted_iota(jnp.int32, (tq, LANES), 1)

    @pl.when(i == 0)
    def _():
        m_ref[...] = jnp.full(m_ref.shape, -jnp.inf, F32)
        l_ref[...] = jnp.zeros(l_ref.shape, F32)
        acc_ref[...] = jnp.zeros(acc_ref.shape, F32)
        for h in range(N_HEADS):
            pair = q_ref[0, :, (h // 2) * LANES:(h // 2 + 1) * LANES]
            mine = (lane < HEAD_DIM) if h % 2 == 0 else (lane >= HEAD_DIM)
            qm_ref[h] = jnp.where(mine, pair, jnp.zeros_like(pair))

    def step(masked):
        if masked:
            kpos = lax.broadcasted_iota(jnp.int32, (tk, tq), 0)
            qpos = lax.broadcasted_iota(jnp.int32, (tk, tq), 1)
            allowed = kpos <= qpos
        for h in range(N_HEADS):
            hs = slice(h * HEAD_DIM, (h + 1) * HEAD_DIM)
            kpair = k_ref[0, :, (h // 2) * LANES:(h // 2 + 1) * LANES]
            s = lax.dot_general(kpair, qm_ref[h], (((1,), (1,)), ((), ())), preferred_element_type=F32)
            s = s + (ct_ref[0, h:h + 1, :] - c_ref[0, :, h:h + 1])
            if masked:
                s = jnp.where(allowed, s, -jnp.inf)
            m_old = m_ref[h:h + 1, :]
            m_new = jnp.maximum(m_old, jnp.max(s, axis=0, keepdims=True))
            alpha = jnp.exp(m_old - m_new)
            p = jnp.exp(s - m_new)
            l_ref[h:h + 1, :] = alpha * l_ref[h:h + 1, :] + jnp.sum(p, axis=0, keepdims=True)
            acc_ref[hs, :] = alpha * acc_ref[hs, :] + jnp.dot(
                vt_ref[0, hs, :], p.astype(BF16), preferred_element_type=F32)
            m_ref[h:h + 1, :] = m_new

    @pl.when(i < j)
    def _():
        step(False)

    @pl.when(i == j)
    def _():
        step(True)
        inv = 1.0 / l_ref[...]
        for h in range(N_HEADS):
            hs = slice(h * HEAD_DIM, (h + 1) * HEAD_DIM)
            acc_ref[hs, :] = acc_ref[hs, :] * inv[h:h + 1, :]
        o_ref[0] = acc_ref[...].T.astype(BF16)


def _prompt_attention(q, kb, vt, c, ct, *, tq):
    b, t, a_dim = q.shape
    tq = min(tq, t)
    nq = t // tq
    kblk = lambda bi, j, i: (bi, jnp.minimum(i, j), 0)
    return pl.pallas_call(
        _prompt_attn_kernel,
        grid=(b, nq, nq),
        in_specs=[pl.BlockSpec((1, tq, a_dim), lambda bi, j, i: (bi, j, 0)),
                  pl.BlockSpec((1, tq, a_dim), kblk),
                  pl.BlockSpec((1, a_dim, tq), lambda bi, j, i: (bi, 0, jnp.minimum(i, j))),
                  pl.BlockSpec((1, tq, LANES), kblk),
                  pl.BlockSpec((1, N_HEADS, tq), lambda bi, j, i: (bi, 0, j))],
        out_specs=pl.BlockSpec((1, tq, a_dim), lambda bi, j, i: (bi, j, 0)),
        out_shape=jax.ShapeDtypeStruct((b, t, a_dim), BF16),
        scratch_shapes=[pltpu.VMEM((N_HEADS, tq, LANES), BF16),
                        pltpu.VMEM((N_HEADS, tq), F32), pltpu.VMEM((N_HEADS, tq), F32),
                        pltpu.VMEM((a_dim, tq), F32)],
        compiler_params=pltpu.CompilerParams(
            dimension_semantics=("parallel", "parallel", "arbitrary"), vmem_limit_bytes=VMEM_LIMIT_BYTES),
        name="prompt_attention",
    )(q, kb, vt, c, ct)


def _sample_attn_kernel(q_ref, pk_ref, pv_ref, kn_ref, vn_ref, cq_ref, cpt_ref, cnt_ref, o_ref,
                        m_ref, l_ref, acc_ref, *, n_past):
    i = pl.program_id(1)
    tq = q_ref.shape[1]

    @pl.when(i == 0)
    def _():
        m_ref[...] = jnp.full(m_ref.shape, -jnp.inf, F32)
        l_ref[...] = jnp.zeros(l_ref.shape, F32)
        acc_ref[...] = jnp.zeros(acc_ref.shape, F32)

    def step(k_all, v_all, ckt, masked):
        tk = k_all.shape[0]
        if masked:
            allowed = (lax.broadcasted_iota(jnp.int32, (tq, tk), 1)
                       <= lax.broadcasted_iota(jnp.int32, (tq, tk), 0))
        for h in range(N_HEADS):
            hs = slice(h * HEAD_DIM, (h + 1) * HEAD_DIM)
            s = lax.dot_general(q_ref[0, :, hs], k_all[:, hs], (((1,), (1,)), ((), ())),
                                preferred_element_type=F32)
            s = s + (cq_ref[0, :, h:h + 1] - ckt[h:h + 1, :])
            if masked:
                s = jnp.where(allowed, s, -jnp.inf)
            m_old = m_ref[:, h:h + 1]
            m_new = jnp.maximum(m_old, jnp.max(s, axis=1, keepdims=True))
            alpha = jnp.exp(m_old - m_new)
            p = jnp.exp(s - m_new)
            l_ref[:, h:h + 1] = alpha * l_ref[:, h:h + 1] + jnp.sum(p, axis=1, keepdims=True)
            acc_ref[:, hs] = alpha * acc_ref[:, hs] + jnp.dot(
                p.astype(BF16), v_all[:, hs], preferred_element_type=F32)
            m_ref[:, h:h + 1] = m_new

    @pl.when(i < n_past)
    def _():
        step(pk_ref[0].astype(BF16), pv_ref[0].astype(BF16), cpt_ref[0], False)

    @pl.when(i == n_past)
    def _():
        step(kn_ref[0], vn_ref[0], cnt_ref[0], True)
        for h in range(N_HEADS):
            hs = slice(h * HEAD_DIM, (h + 1) * HEAD_DIM)
            acc_ref[:, hs] = acc_ref[:, hs] / l_ref[:, h:h + 1]
        o_ref[0] = acc_ref[...].astype(BF16)


def _sample_attention(q, past_k, past_v, kn, vn, cq, cpt, cnt, *, tk):
    b, tq, a_dim = q.shape
    p_len = past_k.shape[1]
    tk = min(tk, p_len)
    n_past = p_len // tk
    past = lambda bi, i: (bi, jnp.minimum(i, n_past - 1), 0)
    whole = lambda bi, i: (bi, 0, 0)
    return pl.pallas_call(
        functools.partial(_sample_attn_kernel, n_past=n_past),
        grid=(b, n_past + 1),
        in_specs=[pl.BlockSpec((1, tq, a_dim), whole),
                  pl.BlockSpec((1, tk, a_dim), past), pl.BlockSpec((1, tk, a_dim), past),
                  pl.BlockSpec((1, tq, a_dim), whole), pl.BlockSpec((1, tq, a_dim), whole),
                  pl.BlockSpec((1, tq, LANES), whole),
                  pl.BlockSpec((1, N_HEADS, tk), lambda bi, i: (bi, 0, jnp.minimum(i, n_past - 1))),
                  pl.BlockSpec((1, N_HEADS, tq), whole)],
        out_specs=pl.BlockSpec((1, tq, a_dim), whole),
        out_shape=jax.ShapeDtypeStruct((b, tq, a_dim), BF16),
        scratch_shapes=[pltpu.VMEM((tq, LANES), F32), pltpu.VMEM((tq, LANES), F32),
                        pltpu.VMEM((tq, a_dim), F32)],
        compiler_params=pltpu.CompilerParams(
            dimension_semantics=("parallel", "arbitrary"), vmem_limit_bytes=VMEM_LIMIT_BYTES),
        name="sample_attention",
    )(q, past_k, past_v, kn, vn, cq, cpt, cnt)


def _merge_ffn_kernel(x_ref, ylru_ref, yatt_ref, nmix_ref, wg_ref, bgate_ref, wbl_ref, wba_ref, wout_ref,
                      nffn_ref, wfi_ref, wfo_ref, nfin_ref, y_ref):
    d = x_ref.shape[1]
    d_ff = wfo_ref.shape[0]
    x = x_ref[...]
    xb = _rmsnorm(x, nmix_ref[...]).astype(BF16)
    g = jax.nn.sigmoid(jnp.dot(xb, wg_ref[...], preferred_element_type=F32) + bgate_ref[...])
    mixed = (g[:, :d] * jnp.dot(ylru_ref[...], wbl_ref[...], preferred_element_type=F32)
             + g[:, d:] * jnp.dot(yatt_ref[...], wba_ref[...], preferred_element_type=F32))
    x = x + jnp.dot(mixed.astype(BF16), wout_ref[...], preferred_element_type=F32)
    xb2 = _rmsnorm(x, nffn_ref[...]).astype(BF16)
    hid = jnp.dot(xb2, wfi_ref[...], preferred_element_type=F32)
    gf, up = hid[:, :d_ff], hid[:, d_ff:]
    act = (gf * jax.nn.sigmoid(gf) * up).astype(BF16)
    x = x + jnp.dot(act, wfo_ref[...], preferred_element_type=F32)
    y_ref[...] = _rmsnorm(x, nfin_ref[...])


def _merge_ffn(x, ylru, yatt, w, norm_final, *, tb):
    n, d = x.shape
    tb = min(tb, n)
    tok = pl.BlockSpec((tb, d), lambda i: (i, 0))
    consts = [w["norm_mix"], w["w_g"], w["b_gate"], w["w_br_lru"], w["w_br_att"], w["w_out"],
              w["norm_ffn"], w["w_ffn_in"], w["w_ffn_out"], norm_final]
    return pl.pallas_call(
        _merge_ffn_kernel,
        grid=(n // tb,),
        in_specs=[tok, tok, tok] + [_const_spec(c.shape) for c in consts],
        out_specs=tok,
        out_shape=jax.ShapeDtypeStruct((n, d), F32),
        compiler_params=pltpu.CompilerParams(
            dimension_semantics=("parallel",), vmem_limit_bytes=VMEM_LIMIT_BYTES),
        name="merge_ffn",
    )(x, ylru, yatt, *consts)


def _prep_layer_weights(norm_mix, w_in, b_forget, b_gate, conv_w, conv_b, w_rg, b_rg, w_ig, b_ig, lam,
                        w_br_lru, w_br_att, w_out, norm_ffn, w_ffn_in, w_ffn_out):
    d = w_in.shape[0]
    a_dim = N_HEADS * HEAD_DIM
    n1 = 2 * d + 3 * a_dim
    row = lambda v: v.reshape(1, -1).astype(F32)
    w_f = jnp.pad(w_in[:, n1 + 2 * d:], ((0, 0), (0, LANES - N_HEADS)))
    per_group = MXU_DIM // (d // N_LRU_BLOCKS)

    def block_diag(wb):
        g = wb.reshape(-1, per_group, wb.shape[1], wb.shape[2])
        eye = jnp.eye(per_group, dtype=wb.dtype)
        return jnp.einsum("gaij,ab->gaibj", g, eye).reshape(g.shape[0], MXU_DIM, MXU_DIM)

    return {
        "norm_mix": row(norm_mix),
        "w_in1": jnp.concatenate([w_in[:, :n1], w_f], axis=1).astype(BF16),
        "w_g": w_in[:, n1:n1 + 2 * d].astype(BF16),
        "b_forget": jnp.pad(row(b_forget), ((0, 0), (0, LANES - N_HEADS))),
        "b_gate": row(b_gate),
        "conv_w": conv_w.astype(F32), "conv_b": row(conv_b),
        "w_gates": jnp.concatenate([block_diag(w_rg), block_diag(w_ig)], axis=2).astype(BF16),
        "b_rg": row(b_rg), "b_ig": row(b_ig), "lam": row(lam),
        "w_br_lru": w_br_lru.astype(BF16), "w_br_att": w_br_att.astype(BF16), "w_out": w_out.astype(BF16),
        "norm_ffn": row(norm_ffn), "w_ffn_in": w_ffn_in.astype(BF16), "w_ffn_out": w_ffn_out.astype(BF16),
    }


def _layer_prompt(x, w, norm_final):
    b, t, d = x.shape
    zeros_conv = jnp.zeros((b, CONV_WIDTH - 1, d), F32)
    zeros_h = jnp.zeros((b, 1, d), F32)
    q, kb, _, vt, kf, vf, logf, ylru, conv_o, h_last = _proj_lru(
        x, zeros_conv, zeros_h, w, reset_first=True, tb=256)
    c, ct = _cumsum_heads(logf, jnp.zeros((b, 1, LANES), F32), tc=512)
    yatt = _prompt_attention(q, kb, vt, c, ct, tq=512)
    y = _merge_ffn(x.reshape(b * t, d), ylru.reshape(b * t, d), yatt.reshape(b * t, d), w, norm_final, tb=256)
    return y.reshape(b, t, d), kf, vf, logf[..., :N_HEADS], conv_o, h_last[:, 0]


def _layer_sample(x, conv0, h0, past_k, past_v, past_logf, w, norm_final):
    b, t, d = x.shape
    p_len = past_k.shape[1]
    a_dim = N_HEADS * HEAD_DIM
    q, kb, vb, _, kf, vf, logf, ylru, conv_o, h_last = _proj_lru(
        x, conv0, h0[:, None, :], w, reset_first=False, tb=t)
    past_lf = jnp.pad(past_logf.astype(F32), ((0, 0), (0, 0), (0, LANES - N_HEADS)))
    c_past, ct_past = _cumsum_heads(past_lf, jnp.zeros((b, 1, LANES), F32), tc=512)
    c_new, ct_new = _cumsum_heads(logf, c_past[:, p_len - 1:p_len, :], tc=t)
    yatt = _sample_attention(q, past_k.reshape(b, p_len, a_dim), past_v.reshape(b, p_len, a_dim), kb, vb,
                             c_new, ct_past, ct_new, tk=512)
    y = _merge_ffn(x.reshape(b * t, d), ylru.reshape(b * t, d), yatt.reshape(b * t, d), w, norm_final, tb=256)
    return y.reshape(b, t, d), kf, vf, logf[..., :N_HEADS], conv_o, h_last[:, 0]


def kernel(x_prompt, x_sample, cache_k, cache_v, cache_logf, state_conv, state_h, norm_mix, w_in, b_forget,
           b_gate, conv_w, conv_b, w_rg, b_rg, w_ig, b_ig, lru_lambda, w_br_lru, w_br_att, w_out, norm_ffn,
           w_ffn_in, w_ffn_out, norm_final):
    depth = norm_mix.shape[0]
    heads = lambda kv: kv.reshape(kv.shape[0], kv.shape[1], N_HEADS, HEAD_DIM)
    nfin = norm_final.reshape(1, -1).astype(F32)
    ones = jnp.ones_like(nfin)
    xp, xs = x_prompt, x_sample
    outs_p, outs_s = [], []
    for l in range(depth):
        w = _prep_layer_weights(norm_mix[l], w_in[l], b_forget[l], b_gate[l], conv_w[l], conv_b[l], w_rg[l],
                                b_rg[l], w_ig[l], b_ig[l], lru_lambda[l], w_br_lru[l], w_br_att[l], w_out[l],
                                norm_ffn[l], w_ffn_in[l], w_ffn_out[l])
        last = l == depth - 1
        assert last, "multi-layer stacks need an un-normalised residual output"
        xp, kp, vp, lfp, cp, hp = _layer_prompt(xp, w, nfin if last else ones)
        xs, ks, vs, lfs, cs, hs = _layer_sample(xs, state_conv[l], state_h[l], cache_k[l], cache_v[l],
                                                cache_logf[l], w, nfin if last else ones)
        outs_p.append((heads(kp), heads(vp), lfp, cp, hp))
        outs_s.append((heads(ks), heads(vs), lfs, cs, hs))
    stack = lambda outs, idx: jnp.stack([o[idx] for o in outs])
    return (xp, xs,
            stack(outs_p, 0), stack(outs_p, 1), stack(outs_p, 2), stack(outs_p, 3), stack(outs_p, 4),
            stack(outs_s, 0), stack(outs_s, 1), stack(outs_s, 2), stack(outs_s, 3), stack(outs_s, 4))
```

```python
import functools
import math

import numpy as np
import jax
import jax.numpy as jnp
from jax import lax
from jax.experimental import pallas as pl
from jax.experimental.pallas import tpu as pltpu

N_HEADS = 16
HEAD_DIM = 64
N_LRU_BLOCKS = 16
CONV_WIDTH = 4
LRU_C = 8.0
EPS = 1e-6

LANES = 128
MXU_DIM = 256
VMEM_LIMIT_BYTES = 56 * 1024 * 1024
CARRY_ROWS = 8

N_PAIRS = N_HEADS // 2
LOG2E = math.log2(math.e)
N_SPLIT = 3
ATTN_BLOCK = 512
TOKEN_BLOCK = 256
BF16_ONE_PAIR = 0x3F803F80

F32 = jnp.float32
BF16 = jnp.bfloat16
U32 = jnp.uint32


def _log_sigmoid(x):
    return jnp.minimum(x, 0.0) - jnp.log1p(jnp.exp(-jnp.abs(x)))


def _gelu_tanh(x):
    c = math.sqrt(2.0 / math.pi)
    return 0.5 * x * (1.0 + jnp.tanh(c * (x + 0.044715 * (x * x * x))))


def _rmsnorm(x, g):
    return x * lax.rsqrt(jnp.mean(x * x, axis=-1, keepdims=True) + EPS) * g


def _scan_affine_rows(a, b):
    n = a.shape[0]
    row = lax.broadcasted_iota(jnp.int32, a.shape, 0)
    d = 1
    while d < n:
        keep = row >= d
        a_sh = jnp.where(keep, pltpu.roll(a, d, 0), 1.0)
        b_sh = jnp.where(keep, pltpu.roll(b, d, 0), 0.0)
        b = a * b_sh + b
        a = a * a_sh
        d *= 2
    return a, b


def _cumsum_rows(x):
    n = x.shape[0]
    row = lax.broadcasted_iota(jnp.int32, x.shape, 0)
    d = 1
    while d < n:
        x = x + jnp.where(row >= d, pltpu.roll(x, d, 0), 0.0)
        d *= 2
    return x


def _const_spec(shape):
    return pl.BlockSpec(shape, lambda *_: (0,) * len(shape), pipeline_mode=pl.Buffered(1))


def _own_lanes(lane, head):
    return (lane < HEAD_DIM) if head % 2 == 0 else (lane >= HEAD_DIM)


def _aug_lane0(head):
    return HEAD_DIM if head % 2 == 0 else 0


def _zero_from(x):
    u = pltpu.bitcast(x, U32)
    return pltpu.bitcast(lax.shift_right_logical(lax.shift_right_logical(u, U32(16)), U32(16)), F32)


def _proj_lru_kernel(x_ref, conv0_ref, h0_ref, nmix_ref, win_ref, convw_ref, convb_ref, wgate_ref,
                     brg_ref, big_ref, lam_ref, bf_ref,
                     q_ref, kb_ref, vb_ref, vt_ref, kf_ref, vf_ref, logf_ref, ylru_ref, convo_ref, hlast_ref,
                     xr_buf, h_carry, *, reset_first):
    t = pl.program_id(1)
    tb, d = x_ref.shape[1], x_ref.shape[2]
    a_dim = kf_ref.shape[2]

    xb = _rmsnorm(x_ref[0], nmix_ref[...]).astype(BF16)

    def proj(lo, width):
        return jnp.dot(xb, win_ref[:, lo:lo + width], preferred_element_type=F32)

    xr = proj(0, d)
    gate = proj(d, d)
    q = proj(2 * d, a_dim)
    k = proj(2 * d + a_dim, a_dim)
    v = proj(2 * d + 2 * a_dim, a_dim)
    f_logit = proj(2 * d + 3 * a_dim, LANES)

    qb = (q * (LOG2E * HEAD_DIM ** -0.5)).astype(BF16)
    kb, vb, vtb = k.astype(BF16), v.astype(BF16), v.T.astype(BF16)
    for hp in range(N_PAIRS):
        ls = slice(hp * LANES, (hp + 1) * LANES)
        q_ref[0, hp] = qb[:, ls]
        kb_ref[0, hp] = kb[:, ls]
        vb_ref[0, hp] = vb[:, ls]
        vt_ref[0, hp] = vtb[ls, :]
    kf_ref[0] = k
    vf_ref[0] = v
    logf_ref[0] = _log_sigmoid(f_logit + bf_ref[...])

    lo = CARRY_ROWS - (CONV_WIDTH - 1)

    @pl.when(t == 0)
    def _():
        xr_buf[lo:CARRY_ROWS, :] = conv0_ref[0]
        h_carry[...] = h0_ref[0]

    xr_buf[CARRY_ROWS:CARRY_ROWS + tb, :] = xr
    xc = convb_ref[...] + xr_buf[lo:lo + tb, :] * convw_ref[0:1, :]
    for j in range(1, CONV_WIDTH):
        xc = xc + xr_buf[lo + j:lo + j + tb, :] * convw_ref[j:j + 1, :]
    tail = xr_buf[tb + lo:tb + CARRY_ROWS, :]
    convo_ref[0] = tail
    xr_buf[lo:CARRY_ROWS, :] = tail

    xcb = xc.astype(BF16)
    parts = [jnp.dot(xcb[:, g * MXU_DIM:(g + 1) * MXU_DIM], wgate_ref[g], preferred_element_type=F32)
             for g in range(d // MXU_DIM)]
    r = jax.nn.sigmoid(jnp.concatenate([p[:, :MXU_DIM] for p in parts], axis=1) + brg_ref[...])
    i = jax.nn.sigmoid(jnp.concatenate([p[:, MXU_DIM:] for p in parts], axis=1) + big_ref[...])
    a = jnp.exp(LRU_C * r * _log_sigmoid(lam_ref[...]))
    mult = jnp.sqrt(1.0 - a * a)
    if reset_first:
        row = lax.broadcasted_iota(jnp.int32, mult.shape, 0)
        mult = jnp.where(jnp.logical_and(row == 0, t == 0), 1.0, mult)
    a_cum, b_cum = _scan_affine_rows(a, mult * i * xc)
    h = a_cum * h_carry[...] + b_cum
    h_carry[...] = h[tb - 1:tb, :]
    hlast_ref[0] = h[tb - 1:tb, :]
    ylru_ref[0] = (h * _gelu_tanh(gate)).astype(BF16)


def _proj_lru(x, conv0, h0, w, *, reset_first):
    b, t, d = x.shape
    a_dim = N_HEADS * HEAD_DIM
    tb = min(TOKEN_BLOCK, t)
    grid = (b, t // tb)
    tok = lambda width: pl.BlockSpec((1, tb, width), lambda bi, ti: (bi, ti, 0))
    per_b = lambda rows: pl.BlockSpec((1, rows, d), lambda bi, ti: (bi, 0, 0))
    pair_major = pl.BlockSpec((1, N_PAIRS, tb, LANES), lambda bi, ti: (bi, 0, ti, 0))
    out_shapes = (
        jax.ShapeDtypeStruct((b, N_PAIRS, t, LANES), BF16),
        jax.ShapeDtypeStruct((b, N_PAIRS, t, LANES), BF16),
        jax.ShapeDtypeStruct((b, N_PAIRS, t, LANES), BF16),
        jax.ShapeDtypeStruct((b, N_PAIRS, LANES, t), BF16),
        jax.ShapeDtypeStruct((b, t, a_dim), F32),
        jax.ShapeDtypeStruct((b, t, a_dim), F32),
        jax.ShapeDtypeStruct((b, t, LANES), F32),
        jax.ShapeDtypeStruct((b, t, d), BF16),
        jax.ShapeDtypeStruct((b, CONV_WIDTH - 1, d), F32),
        jax.ShapeDtypeStruct((b, 1, d), F32),
    )
    out_specs = (
        pair_major, pair_major, pair_major,
        pl.BlockSpec((1, N_PAIRS, LANES, tb), lambda bi, ti: (bi, 0, 0, ti)),
        tok(a_dim), tok(a_dim), tok(LANES), tok(d),
        per_b(CONV_WIDTH - 1), per_b(1),
    )
    in_specs = [
        tok(d), per_b(CONV_WIDTH - 1), per_b(1),
        _const_spec(w["norm_mix"].shape), _const_spec(w["w_in1"].shape),
        _const_spec(w["conv_w"].shape), _const_spec(w["conv_b"].shape), _const_spec(w["w_gates"].shape),
        _const_spec(w["b_rg"].shape), _const_spec(w["b_ig"].shape), _const_spec(w["lam"].shape),
        _const_spec(w["b_forget"].shape),
    ]
    return pl.pallas_call(
        functools.partial(_proj_lru_kernel, reset_first=reset_first),
        grid=grid, in_specs=in_specs, out_specs=out_specs, out_shape=out_shapes,
        scratch_shapes=[pltpu.VMEM((CARRY_ROWS + tb, d), F32), pltpu.VMEM((1, d), F32)],
        compiler_params=pltpu.CompilerParams(
            dimension_semantics=("arbitrary", "arbitrary"), vmem_limit_bytes=VMEM_LIMIT_BYTES),
        name="proj_lru",
    )(x, conv0, h0, w["norm_mix"], w["w_in1"], w["conv_w"], w["conv_b"], w["w_gates"],
      w["b_rg"], w["b_ig"], w["lam"], w["b_forget"])


def _bias_placement():
    place = np.zeros((N_SPLIT * LANES, N_PAIRS * LANES), np.float32)
    for h in range(N_HEADS):
        for s in range(N_SPLIT):
            place[s * LANES + h, (h // 2) * LANES + _aug_lane0(h) + s] = 1.0
    return jnp.asarray(place, BF16)


def _cumsum_kernel(lf_ref, off_ref, place_ref, c_ref, ct_ref, kx_ref, base_ref, carry):
    @pl.when(pl.program_id(1) == 0)
    def _():
        carry[...] = off_ref[0]

    n = lf_ref.shape[1]
    c = _cumsum_rows(lf_ref[0]) + carry[...]
    carry[...] = c[n - 1:n, :]
    c_ref[0] = c
    ct_ref[0] = c.T
    base = c[0:1, :]
    base_ref[0, 0] = base * LOG2E
    rest = (base - c) * LOG2E
    pieces = []
    for _ in range(N_SPLIT):
        piece = rest.astype(BF16)
        pieces.append(piece)
        rest = rest - piece.astype(F32)
    placed = jnp.dot(jnp.concatenate(pieces, axis=1), place_ref[...], preferred_element_type=F32)
    for hp in range(N_PAIRS):
        kx_ref[0, hp] = placed[:, hp * LANES:(hp + 1) * LANES].astype(BF16)


def _cumsum_heads(logf, offset):
    b, t, _ = logf.shape
    tc = min(ATTN_BLOCK, t)
    place = _bias_placement()
    return pl.pallas_call(
        _cumsum_kernel,
        grid=(b, t // tc),
        in_specs=[pl.BlockSpec((1, tc, LANES), lambda bi, ti: (bi, ti, 0)),
                  pl.BlockSpec((1, 1, LANES), lambda bi, ti: (bi, 0, 0)),
                  _const_spec(place.shape)],
        out_specs=(pl.BlockSpec((1, tc, LANES), lambda bi, ti: (bi, ti, 0)),
                   pl.BlockSpec((1, LANES, tc), lambda bi, ti: (bi, 0, ti)),
                   pl.BlockSpec((1, N_PAIRS, tc, LANES), lambda bi, ti: (bi, 0, ti, 0)),
                   pl.BlockSpec((1, 1, 1, LANES), lambda bi, ti: (bi, ti, 0, 0))),
        out_shape=(jax.ShapeDtypeStruct((b, t, LANES), F32), jax.ShapeDtypeStruct((b, LANES, t), F32),
                   jax.ShapeDtypeStruct((b, N_PAIRS, t, LANES), BF16),
                   jax.ShapeDtypeStruct((b, t // tc, 1, LANES), F32)),
        scratch_shapes=[pltpu.VMEM((1, LANES), F32)],
        compiler_params=pltpu.CompilerParams(dimension_semantics=("arbitrary", "arbitrary")),
        name="cumsum_heads",
    )(logf, offset, place)


def _prompt_attn_kernel(base_ref, q_ref, k_ref, kx_ref, vt_ref, o_ref, qa_ref, m_ref, l_ref, acc_ref):
    bi, j, i = pl.program_id(0), pl.program_id(1), pl.program_id(2)
    nk = pl.num_programs(2)
    tq, tk = q_ref.shape[2], k_ref.shape[2]
    lane_q = lax.broadcasted_iota(jnp.int32, (tq // 2, LANES), 1)
    lane_k = lax.broadcasted_iota(jnp.int32, (tk // 2, LANES), 1)

    @pl.when(i == 0)
    def _():
        m_ref[...] = jnp.full(m_ref.shape, -jnp.inf, F32)
        l_ref[...] = jnp.zeros(l_ref.shape, F32)
        acc_ref[...] = jnp.zeros(acc_ref.shape, F32)
        for h in range(N_HEADS):
            pair = pltpu.bitcast(q_ref[0, h // 2], U32)
            a0 = _aug_lane0(h)
            ones = jnp.where(jnp.logical_and(lane_q >= a0, lane_q < a0 + N_SPLIT), U32(BF16_ONE_PAIR), U32(0))
            qa_ref[h] = pltpu.bitcast(jnp.where(_own_lanes(lane_q, h), pair, ones), BF16)

    def scores(h):
        pair = pltpu.bitcast(k_ref[0, h // 2], U32)
        bias = pltpu.bitcast(kx_ref[0, h // 2], U32)
        ka = pltpu.bitcast(jnp.where(_own_lanes(lane_k, h), pair, bias), BF16)
        return lax.dot_general(ka, qa_ref[h], (((1,), (1,)), ((), ())), preferred_element_type=F32)

    def step(masked):
        if masked:
            allowed = (lax.broadcasted_iota(jnp.int32, (tk, tq), 0) <= lax.broadcasted_iota(jnp.int32, (tk, tq), 1))
        s_next = scores(0)
        for h in range(N_HEADS):
            s = s_next
            if h + 1 < N_HEADS:
                s_next = scores(h + 1)
            if masked:
                s = jnp.where(allowed, s, -jnp.inf)
            base = base_ref[(bi * nk + i) * N_HEADS + h]
            m_old = m_ref[h:h + 1, :]
            m_new = jnp.maximum(m_old, jnp.max(s, axis=0, keepdims=True) - base)
            alpha = jnp.exp2(m_old - m_new)
            p = jnp.exp2(s - (m_new + base))
            l_ref[h:h + 1, :] = alpha * l_ref[h:h + 1, :] + jnp.sum(p, axis=0, keepdims=True)
            if h + 1 < N_HEADS:
                alpha = alpha + _zero_from(s_next[tk - 8:tk, tq - LANES:tq])[0:1, 0:1]
            hs = slice((h % 2) * HEAD_DIM, (h % 2 + 1) * HEAD_DIM)
            acc_ref[h // 2, hs, :] = alpha * acc_ref[h // 2, hs, :] + jnp.dot(
                vt_ref[0, h // 2, hs, :], p.astype(BF16), preferred_element_type=F32)
            m_ref[h:h + 1, :] = m_new

    @pl.when(i < j)
    def _():
        step(False)

    @pl.when(i == j)
    def _():
        step(True)
        inv = 1.0 / l_ref[...]
        for hp in range(N_PAIRS):
            scale = jnp.concatenate([jnp.broadcast_to(inv[2 * hp + e:2 * hp + e + 1, :], (HEAD_DIM, tq))
                                     for e in range(2)], axis=0)
            o_ref[0, :, hp * LANES:(hp + 1) * LANES] = (acc_ref[hp] * scale).T.astype(BF16)


def _prompt_attention(q, kb, kx, vt, base):
    b, _, t, _ = q.shape
    tq = min(ATTN_BLOCK, t)
    nq = t // tq
    kblk = lambda bi, j, i, *_: (bi, 0, jnp.minimum(i, j), 0)
    grid_spec = pltpu.PrefetchScalarGridSpec(
        num_scalar_prefetch=1, grid=(b, nq, nq),
        in_specs=[pl.BlockSpec((1, N_PAIRS, tq, LANES), lambda bi, j, i, *_: (bi, 0, j, 0)),
                  pl.BlockSpec((1, N_PAIRS, tq, LANES), kblk),
                  pl.BlockSpec((1, N_PAIRS, tq, LANES), kblk),
                  pl.BlockSpec((1, N_PAIRS, LANES, tq), lambda bi, j, i, *_: (bi, 0, 0, jnp.minimum(i, j)))],
        out_specs=pl.BlockSpec((1, tq, N_PAIRS * LANES), lambda bi, j, i, *_: (bi, j, 0)),
        scratch_shapes=[pltpu.VMEM((N_HEADS, tq, LANES), BF16),
                        pltpu.VMEM((N_HEADS, tq), F32), pltpu.VMEM((N_HEADS, tq), F32),
                        pltpu.VMEM((N_PAIRS, LANES, tq), F32)])
    return pl.pallas_call(
        _prompt_attn_kernel, grid_spec=grid_spec,
        out_shape=jax.ShapeDtypeStruct((b, t, N_PAIRS * LANES), BF16),
        compiler_params=pltpu.CompilerParams(
            dimension_semantics=("parallel", "parallel", "arbitrary"), vmem_limit_bytes=VMEM_LIMIT_BYTES),
        name="prompt_attention",
    )(base, q, kb, kx, vt)


def _sample_attn_kernel(q_ref, pk_ref, pv_ref, kn_ref, vn_ref, cpt_ref, cnt_ref, o_ref,
                        qw_ref, m_ref, l_ref, acc_ref, *, n_past):
    i = pl.program_id(1)
    tq = q_ref.shape[2]
    rows = 2 * tq

    @pl.when(i == 0)
    def _():
        m_ref[...] = jnp.full(m_ref.shape, -jnp.inf, F32)
        l_ref[...] = jnp.zeros(l_ref.shape, F32)
        acc_ref[...] = jnp.zeros(acc_ref.shape, F32)
        lane = lax.broadcasted_iota(jnp.int32, (tq // 2, LANES), 1)
        for hp in range(N_PAIRS):
            pair = pltpu.bitcast(q_ref[0, hp], U32)
            for e in range(2):
                qw_ref[hp, e * tq:(e + 1) * tq, :] = pltpu.bitcast(
                    jnp.where(_own_lanes(lane, e), pair, U32(0)), BF16)

    def step(k_of, v_of, ckt, masked):
        tk = ckt.shape[1]
        s = jnp.concatenate(
            [lax.dot_general(qw_ref[hp], k_of(hp), (((1,), (1,)), ((), ())), preferred_element_type=F32)
             for hp in range(N_PAIRS)], axis=0)
        ck = jnp.concatenate([jnp.broadcast_to(ckt[h:h + 1, :], (tq, tk)) for h in range(N_HEADS)], axis=0)
        s = s - ck * LOG2E
        if masked:
            q_idx = lax.rem(lax.broadcasted_iota(jnp.int32, s.shape, 0), tq)
            s = jnp.where(lax.broadcasted_iota(jnp.int32, s.shape, 1) <= q_idx, s, -jnp.inf)
        m_old = m_ref[...]
        m_new = jnp.maximum(m_old, jnp.max(s, axis=1, keepdims=True))
        alpha = jnp.exp2(m_old - m_new)
        p = jnp.exp2(s - m_new)
        l_ref[...] = alpha * l_ref[...] + jnp.sum(p, axis=1, keepdims=True)
        m_ref[...] = m_new
        pb = p.astype(BF16)
        for hp in range(N_PAIRS):
            rs = slice(hp * rows, (hp + 1) * rows)
            acc_ref[hp] = alpha[rs, :] * acc_ref[hp] + jnp.dot(pb[rs, :], v_of(hp), preferred_element_type=F32)

    pair_lanes = lambda hp: slice(hp * LANES, (hp + 1) * LANES)

    @pl.when(i < n_past)
    def _():
        step(lambda hp: pk_ref[0, :, pair_lanes(hp)].astype(BF16),
             lambda hp: pv_ref[0, :, pair_lanes(hp)].astype(BF16), cpt_ref[0], False)

    @pl.when(i == n_past)
    def _():
        step(lambda hp: kn_ref[0, hp], lambda hp: vn_ref[0, hp], cnt_ref[0], True)
        inv = 1.0 / l_ref[...]
        for hp in range(N_PAIRS):
            o = acc_ref[hp] * inv[hp * rows:(hp + 1) * rows, :]
            for e in range(2):
                lo = hp * LANES + e * HEAD_DIM
                o_ref[0, :, lo:lo + HEAD_DIM] = o[e * tq:(e + 1) * tq, e * HEAD_DIM:(e + 1) * HEAD_DIM].astype(BF16)


def _sample_attention(q, past_k, past_v, kn, vn, cpt, cnt):
    b, _, tq, _ = q.shape
    p_len, a_dim = past_k.shape[1], past_k.shape[2]
    tk = min(ATTN_BLOCK, p_len)
    n_past = p_len // tk
    past = lambda bi, i: (bi, jnp.minimum(i, n_past - 1), 0)
    whole4 = pl.BlockSpec((1, N_PAIRS, tq, LANES), lambda bi, i: (bi, 0, 0, 0))
    return pl.pallas_call(
        functools.partial(_sample_attn_kernel, n_past=n_past),
        grid=(b, n_past + 1),
        in_specs=[whole4,
                  pl.BlockSpec((1, tk, a_dim), past), pl.BlockSpec((1, tk, a_dim), past),
                  whole4, whole4,
                  pl.BlockSpec((1, N_HEADS, tk), lambda bi, i: (bi, 0, jnp.minimum(i, n_past - 1))),
                  pl.BlockSpec((1, N_HEADS, tq), lambda bi, i: (bi, 0, 0))],
        out_specs=pl.BlockSpec((1, tq, a_dim), lambda bi, i: (bi, 0, 0)),
        out_shape=jax.ShapeDtypeStruct((b, tq, a_dim), BF16),
        scratch_shapes=[pltpu.VMEM((N_PAIRS, 2 * tq, LANES), BF16),
                        pltpu.VMEM((N_HEADS * tq, 1), F32), pltpu.VMEM((N_HEADS * tq, 1), F32),
                        pltpu.VMEM((N_PAIRS, 2 * tq, LANES), F32)],
        compiler_params=pltpu.CompilerParams(
            dimension_semantics=("parallel", "arbitrary"), vmem_limit_bytes=VMEM_LIMIT_BYTES),
        name="sample_attention",
    )(q, past_k, past_v, kn, vn, cpt, cnt)


def _merge_ffn_kernel(x_ref, ylru_ref, yatt_ref, nmix_ref, wg_ref, bgate_ref, wbl_ref, wba_ref, wout_ref,
                      nffn_ref, wfi_ref, wfo_ref, nfin_ref, y_ref):
    d = x_ref.shape[1]
    d_ff = wfo_ref.shape[0]
    x = x_ref[...]
    xb = _rmsnorm(x, nmix_ref[...]).astype(BF16)
    g = jax.nn.sigmoid(jnp.dot(xb, wg_ref[...], preferred_element_type=F32) + bgate_ref[...])
    mixed = (g[:, :d] * jnp.dot(ylru_ref[...], wbl_ref[...], preferred_element_type=F32)
             + g[:, d:] * jnp.dot(yatt_ref[...], wba_ref[...], preferred_element_type=F32))
    x = x + jnp.dot(mixed.astype(BF16), wout_ref[...], preferred_element_type=F32)
    xb2 = _rmsnorm(x, nffn_ref[...]).astype(BF16)
    hid = jnp.dot(xb2, wfi_ref[...], preferred_element_type=F32)
    gf, up = hid[:, :d_ff], hid[:, d_ff:]
    act = (gf * jax.nn.sigmoid(gf) * up).astype(BF16)
    x = x + jnp.dot(act, wfo_ref[...], preferred_element_type=F32)
    y_ref[...] = _rmsnorm(x, nfin_ref[...])


def _merge_ffn(x, ylru, yatt, w, norm_final):
    n, d = x.shape
    tb = min(TOKEN_BLOCK, n)
    tok = pl.BlockSpec((tb, d), lambda i: (i, 0))
    consts = [w["norm_mix"], w["w_g"], w["b_gate"], w["w_br_lru"], w["w_br_att"], w["w_out"],
              w["norm_ffn"], w["w_ffn_in"], w["w_ffn_out"], norm_final]
    return pl.pallas_call(
        _merge_ffn_kernel,
        grid=(n // tb,),
        in_specs=[tok, tok, tok] + [_const_spec(c.shape) for c in consts],
        out_specs=tok,
        out_shape=jax.ShapeDtypeStruct((n, d), F32),
        compiler_params=pltpu.CompilerParams(
            dimension_semantics=("parallel",), vmem_limit_bytes=VMEM_LIMIT_BYTES),
        name="merge_ffn",
    )(x, ylru, yatt, *consts)


def _prep_layer_weights(norm_mix, w_in, b_forget, b_gate, conv_w, conv_b, w_rg, b_rg, w_ig, b_ig, lam,
                        w_br_lru, w_br_att, w_out, norm_ffn, w_ffn_in, w_ffn_out):
    d = w_in.shape[0]
    a_dim = N_HEADS * HEAD_DIM
    n1 = 2 * d + 3 * a_dim
    row = lambda v: v.reshape(1, -1).astype(F32)
    w_f = jnp.pad(w_in[:, n1 + 2 * d:], ((0, 0), (0, LANES - N_HEADS)))
    per_group = MXU_DIM // (d // N_LRU_BLOCKS)

    def block_diag(wb):
        g = wb.reshape(-1, per_group, wb.shape[1], wb.shape[2])
        eye = jnp.eye(per_group, dtype=wb.dtype)
        return jnp.einsum("gaij,ab->gaibj", g, eye).reshape(g.shape[0], MXU_DIM, MXU_DIM)

    return {
        "norm_mix": row(norm_mix),
        "w_in1": jnp.concatenate([w_in[:, :n1], w_f], axis=1).astype(BF16),
        "w_g": w_in[:, n1:n1 + 2 * d].astype(BF16),
        "b_forget": jnp.pad(row(b_forget), ((0, 0), (0, LANES - N_HEADS))),
        "b_gate": row(b_gate),
        "conv_w": conv_w.astype(F32), "conv_b": row(conv_b),
        "w_gates": jnp.concatenate([block_diag(w_rg), block_diag(w_ig)], axis=2).astype(BF16),
        "b_rg": row(b_rg), "b_ig": row(b_ig), "lam": row(lam),
        "w_br_lru": w_br_lru.astype(BF16), "w_br_att": w_br_att.astype(BF16), "w_out": w_out.astype(BF16),
        "norm_ffn": row(norm_ffn), "w_ffn_in": w_ffn_in.astype(BF16), "w_ffn_out": w_ffn_out.astype(BF16),
    }


def _layer_prompt(x, w, norm_final):
    b, t, d = x.shape
    zeros_conv = jnp.zeros((b, CONV_WIDTH - 1, d), F32)
    zeros_h = jnp.zeros((b, 1, d), F32)
    q, kb, _, vt, kf, vf, logf, ylru, conv_o, h_last = _proj_lru(x, zeros_conv, zeros_h, w, reset_first=True)
    _, _, kx, base = _cumsum_heads(logf, jnp.zeros((b, 1, LANES), F32))
    yatt = _prompt_attention(q, kb, kx, vt, base[:, :, 0, :N_HEADS].reshape(-1))
    y = _merge_ffn(x.reshape(b * t, d), ylru.reshape(b * t, d), yatt.reshape(b * t, d), w, norm_final)
    return y.reshape(b, t, d), kf, vf, logf[..., :N_HEADS], conv_o, h_last[:, 0]


def _layer_sample(x, conv0, h0, past_k, past_v, past_logf, w, norm_final):
    b, t, d = x.shape
    p_len = past_k.shape[1]
    a_dim = N_HEADS * HEAD_DIM
    q, kb, vb, _, kf, vf, logf, ylru, conv_o, h_last = _proj_lru(x, conv0, h0[:, None, :], w, reset_first=False)
    past_lf = jnp.pad(past_logf.astype(F32), ((0, 0), (0, 0), (0, LANES - N_HEADS)))
    c_past, ct_past, _, _ = _cumsum_heads(past_lf, jnp.zeros((b, 1, LANES), F32))
    _, ct_new, _, _ = _cumsum_heads(logf, c_past[:, p_len - 1:p_len, :])
    yatt = _sample_attention(q, past_k.reshape(b, p_len, a_dim), past_v.reshape(b, p_len, a_dim), kb, vb,
                             ct_past, ct_new)
    y = _merge_ffn(x.reshape(b * t, d), ylru.reshape(b * t, d), yatt.reshape(b * t, d), w, norm_final)
    return y.reshape(b, t, d), kf, vf, logf[..., :N_HEADS], conv_o, h_last[:, 0]


def kernel(x_prompt, x_sample, cache_k, cache_v, cache_logf, state_conv, state_h, norm_mix, w_in, b_forget,
           b_gate, conv_w, conv_b, w_rg, b_rg, w_ig, b_ig, lru_lambda, w_br_lru, w_br_att, w_out, norm_ffn,
           w_ffn_in, w_ffn_out, norm_final):
    assert norm_mix.shape[0] == 1, "single-layer trunk: the final rmsnorm is fused into the layer's last stage"
    heads = lambda kv: kv.reshape(kv.shape[0], kv.shape[1], N_HEADS, HEAD_DIM)[None]
    nfin = norm_final.reshape(1, -1).astype(F32)
    w = _prep_layer_weights(norm_mix[0], w_in[0], b_forget[0], b_gate[0], conv_w[0], conv_b[0], w_rg[0], b_rg[0],
                            w_ig[0], b_ig[0], lru_lambda[0], w_br_lru[0], w_br_att[0], w_out[0], norm_ffn[0],
                            w_ffn_in[0], w_ffn_out[0])
    yp, kp, vp, lfp, cp, hp = _layer_prompt(x_prompt, w, nfin)
    ys, ks, vs, lfs, cs, hs = _layer_sample(x_sample, state_conv[0], state_h[0], cache_k[0], cache_v[0],
                                            cache_logf[0], w, nfin)
    return (yp, ys, heads(kp), heads(vp), lfp[None], cp[None], hp[None],
            heads(ks), heads(vs), lfs[None], cs[None], hs[None])
```

```python
import functools
import math

import numpy as np
import jax
import jax.numpy as jnp
from jax import lax
from jax.experimental import pallas as pl
from jax.experimental.pallas import tpu as pltpu

N_HEADS = 16
HEAD_DIM = 64
N_LRU_BLOCKS = 16
CONV_WIDTH = 4
LRU_C = 8.0
EPS = 1e-6

LANES = 128
MXU_DIM = 256
VMEM_LIMIT_BYTES = 56 * 1024 * 1024
CARRY_ROWS = 8

N_PAIRS = N_HEADS // 2
LOG2E = math.log2(math.e)
N_SPLIT = 3
ATTN_BLOCK = 512
TOKEN_BLOCK = 256

STAT_BASE, STAT_BMAX, STAT_KNORM, STAT_QNORM, N_STATS = 0, 1, 2, 3, 4
STAT_ROWS = 8
NORM_SLACK = 1.01
EXP2_ZERO_BELOW = -150.0
SKIP_MARGIN = 4.0

F32 = jnp.float32
BF16 = jnp.bfloat16
U32 = jnp.uint32


def _log_sigmoid(x):
    return jnp.minimum(x, 0.0) - jnp.log1p(jnp.exp(-jnp.abs(x)))


def _gelu_tanh(x):
    c = math.sqrt(2.0 / math.pi)
    return 0.5 * x * (1.0 + jnp.tanh(c * (x + 0.044715 * (x * x * x))))


def _rmsnorm(x, g):
    return x * lax.rsqrt(jnp.mean(x * x, axis=-1, keepdims=True) + EPS) * g


def _scan_affine_rows(a, b):
    n = a.shape[0]
    row = lax.broadcasted_iota(jnp.int32, a.shape, 0)
    d = 1
    while d < n:
        keep = row >= d
        a_sh = jnp.where(keep, pltpu.roll(a, d, 0), 1.0)
        b_sh = jnp.where(keep, pltpu.roll(b, d, 0), 0.0)
        b = a * b_sh + b
        a = a * a_sh
        d *= 2
    return a, b


def _cumsum_rows(x):
    n = x.shape[0]
    row = lax.broadcasted_iota(jnp.int32, x.shape, 0)
    d = 1
    while d < n:
        x = x + jnp.where(row >= d, pltpu.roll(x, d, 0), 0.0)
        d *= 2
    return x


def _const_spec(shape):
    return pl.BlockSpec(shape, lambda *_: (0,) * len(shape), pipeline_mode=pl.Buffered(1))


def _own_lanes(lane, head):
    return (lane < HEAD_DIM) if head % 2 == 0 else (lane >= HEAD_DIM)


def _aug_lane0(head):
    return HEAD_DIM if head % 2 == 0 else 0


def _zero_from(x):
    u = pltpu.bitcast(x, U32)
    return pltpu.bitcast(lax.shift_right_logical(lax.shift_right_logical(u, U32(16)), U32(16)), F32)


def _proj_lru_kernel(x_ref, conv0_ref, h0_ref, nmix_ref, win_ref, convw_ref, convb_ref, wgate_ref,
                     brg_ref, big_ref, lam_ref, bf_ref,
                     q_ref, kb_ref, vb_ref, vt_ref, kf_ref, vf_ref, logf_ref, ylru_ref, convo_ref, hlast_ref,
                     xr_buf, h_carry, *, reset_first):
    t = pl.program_id(1)
    tb, d = x_ref.shape[1], x_ref.shape[2]
    a_dim = kf_ref.shape[2]

    xb = _rmsnorm(x_ref[0], nmix_ref[...]).astype(BF16)

    def proj(lo, width):
        return jnp.dot(xb, win_ref[:, lo:lo + width], preferred_element_type=F32)

    xr = proj(0, d)
    gate = proj(d, d)
    q = proj(2 * d, a_dim)
    k = proj(2 * d + a_dim, a_dim)
    v = proj(2 * d + 2 * a_dim, a_dim)
    f_logit = proj(2 * d + 3 * a_dim, LANES)

    qb = (q * (LOG2E * HEAD_DIM ** -0.5)).astype(BF16)
    kb, vb, vtb = k.astype(BF16), v.astype(BF16), v.T.astype(BF16)
    for hp in range(N_PAIRS):
        ls = slice(hp * LANES, (hp + 1) * LANES)
        q_ref[0, hp] = qb[:, ls]
        kb_ref[0, hp] = kb[:, ls]
        vb_ref[0, hp] = vb[:, ls]
        vt_ref[0, hp] = vtb[ls, :]
    kf_ref[0] = k
    vf_ref[0] = v
    logf_ref[0] = _log_sigmoid(f_logit + bf_ref[...])

    lo = CARRY_ROWS - (CONV_WIDTH - 1)

    @pl.when(t == 0)
    def _():
        xr_buf[lo:CARRY_ROWS, :] = conv0_ref[0]
        h_carry[...] = h0_ref[0]

    xr_buf[CARRY_ROWS:CARRY_ROWS + tb, :] = xr
    xc = convb_ref[...] + xr_buf[lo:lo + tb, :] * convw_ref[0:1, :]
    for j in range(1, CONV_WIDTH):
        xc = xc + xr_buf[lo + j:lo + j + tb, :] * convw_ref[j:j + 1, :]
    tail = xr_buf[tb + lo:tb + CARRY_ROWS, :]
    convo_ref[0] = tail
    xr_buf[lo:CARRY_ROWS, :] = tail

    xcb = xc.astype(BF16)
    parts = [jnp.dot(xcb[:, g * MXU_DIM:(g + 1) * MXU_DIM], wgate_ref[g], preferred_element_type=F32)
             for g in range(d // MXU_DIM)]
    r = jax.nn.sigmoid(jnp.concatenate([p[:, :MXU_DIM] for p in parts], axis=1) + brg_ref[...])
    i = jax.nn.sigmoid(jnp.concatenate([p[:, MXU_DIM:] for p in parts], axis=1) + big_ref[...])
    a = jnp.exp(LRU_C * r * _log_sigmoid(lam_ref[...]))
    mult = jnp.sqrt(1.0 - a * a)
    if reset_first:
        row = lax.broadcasted_iota(jnp.int32, mult.shape, 0)
        mult = jnp.where(jnp.logical_and(row == 0, t == 0), 1.0, mult)
    a_cum, b_cum = _scan_affine_rows(a, mult * i * xc)
    h = a_cum * h_carry[...] + b_cum
    h_carry[...] = h[tb - 1:tb, :]
    hlast_ref[0] = h[tb - 1:tb, :]
    ylru_ref[0] = (h * _gelu_tanh(gate)).astype(BF16)


def _proj_lru(x, conv0, h0, w, *, reset_first):
    b, t, d = x.shape
    a_dim = N_HEADS * HEAD_DIM
    tb = min(TOKEN_BLOCK, t)
    grid = (b, t // tb)
    tok = lambda width: pl.BlockSpec((1, tb, width), lambda bi, ti: (bi, ti, 0))
    per_b = lambda rows: pl.BlockSpec((1, rows, d), lambda bi, ti: (bi, 0, 0))
    pair_major = pl.BlockSpec((1, N_PAIRS, tb, LANES), lambda bi, ti: (bi, 0, ti, 0))
    out_shapes = (
        jax.ShapeDtypeStruct((b, N_PAIRS, t, LANES), BF16),
        jax.ShapeDtypeStruct((b, N_PAIRS, t, LANES), BF16),
        jax.ShapeDtypeStruct((b, N_PAIRS, t, LANES), BF16),
        jax.ShapeDtypeStruct((b, N_PAIRS, LANES, t), BF16),
        jax.ShapeDtypeStruct((b, t, a_dim), F32),
        jax.ShapeDtypeStruct((b, t, a_dim), F32),
        jax.ShapeDtypeStruct((b, t, LANES), F32),
        jax.ShapeDtypeStruct((b, t, d), BF16),
        jax.ShapeDtypeStruct((b, CONV_WIDTH - 1, d), F32),
        jax.ShapeDtypeStruct((b, 1, d), F32),
    )
    out_specs = (
        pair_major, pair_major, pair_major,
        pl.BlockSpec((1, N_PAIRS, LANES, tb), lambda bi, ti: (bi, 0, 0, ti)),
        tok(a_dim), tok(a_dim), tok(LANES), tok(d),
        per_b(CONV_WIDTH - 1), per_b(1),
    )
    in_specs = [
        tok(d), per_b(CONV_WIDTH - 1), per_b(1),
        _const_spec(w["norm_mix"].shape), _const_spec(w["w_in1"].shape),
        _const_spec(w["conv_w"].shape), _const_spec(w["conv_b"].shape), _const_spec(w["w_gates"].shape),
        _const_spec(w["b_rg"].shape), _const_spec(w["b_ig"].shape), _const_spec(w["lam"].shape),
        _const_spec(w["b_forget"].shape),
    ]
    return pl.pallas_call(
        functools.partial(_proj_lru_kernel, reset_first=reset_first),
        grid=grid, in_specs=in_specs, out_specs=out_specs, out_shape=out_shapes,
        scratch_shapes=[pltpu.VMEM((CARRY_ROWS + tb, d), F32), pltpu.VMEM((1, d), F32)],
        compiler_params=pltpu.CompilerParams(
            dimension_semantics=("arbitrary", "arbitrary"), vmem_limit_bytes=VMEM_LIMIT_BYTES),
        name="proj_lru",
    )(x, conv0, h0, w["norm_mix"], w["w_in1"], w["conv_w"], w["conv_b"], w["w_gates"],
      w["b_rg"], w["b_ig"], w["lam"], w["b_forget"])


def _bias_placement():
    place = np.zeros((N_SPLIT * LANES, N_PAIRS * LANES), np.float32)
    for h in range(N_HEADS):
        for s in range(N_SPLIT):
            place[s * LANES + h, (h // 2) * LANES + _aug_lane0(h) + s] = 1.0
    return jnp.asarray(place, BF16)


def _head_lane_sum():
    sel = np.zeros((N_PAIRS * LANES, LANES), np.float32)
    for h in range(N_HEADS):
        lo = (h // 2) * LANES + (h % 2) * HEAD_DIM
        sel[lo:lo + HEAD_DIM, h] = 1.0
    return jnp.asarray(sel, BF16)


def _query_ones():
    ones = np.zeros((ATTN_BLOCK, LANES), np.float32)
    for h in range(2):
        ones[:, _aug_lane0(h):_aug_lane0(h) + N_SPLIT] = 1.0
    return jnp.asarray(ones, BF16)


def _max_head_norm(slabs_ref, sel_ref):
    sq = jnp.concatenate([jnp.square(slabs_ref[0, hp].astype(F32)) for hp in range(N_PAIRS)], axis=1)
    norm2 = jnp.dot(sq.astype(BF16), sel_ref[...], preferred_element_type=F32)
    return jnp.sqrt(jnp.max(norm2, axis=0, keepdims=True) * NORM_SLACK)


def _attn_prep_kernel(lf_ref, k_ref, q_ref, place_ref, sel_ref, ka_ref, st_ref, carry):
    @pl.when(pl.program_id(1) == 0)
    def _():
        carry[...] = jnp.zeros(carry.shape, F32)

    n = lf_ref.shape[1]
    c = _cumsum_rows(lf_ref[0]) + carry[...]
    carry[...] = c[n - 1:n, :]
    base = c[0:1, :]
    rest = (base - c) * LOG2E
    pieces = []
    for _ in range(N_SPLIT):
        piece = rest.astype(BF16)
        pieces.append(piece)
        rest = rest - piece.astype(F32)
    placed = jnp.dot(jnp.concatenate(pieces, axis=1), place_ref[...], preferred_element_type=F32)
    lane = lax.broadcasted_iota(jnp.int32, (n, LANES), 1)
    for h in range(N_HEADS):
        bias = placed[:, (h // 2) * LANES:(h // 2 + 1) * LANES].astype(BF16)
        ka_ref[0, h] = jnp.where(_own_lanes(lane, h), k_ref[0, h // 2], bias)

    stats = {STAT_BASE: base * LOG2E,
             STAT_BMAX: c[n - 1:n, :] * -LOG2E,
             STAT_KNORM: _max_head_norm(k_ref, sel_ref),
             STAT_QNORM: _max_head_norm(q_ref, sel_ref)}
    rows = [stats.get(r, jnp.zeros((1, LANES), F32)) for r in range(STAT_ROWS)]
    st_ref[0, 0] = jnp.concatenate(rows, axis=0)


def _attn_prep(logf, kb, q):
    b, t, _ = logf.shape
    tc = min(ATTN_BLOCK, t)
    place, sel = _bias_placement(), _head_lane_sum()
    pair_major = pl.BlockSpec((1, N_PAIRS, tc, LANES), lambda bi, ti: (bi, 0, ti, 0))
    return pl.pallas_call(
        _attn_prep_kernel,
        grid=(b, t // tc),
        in_specs=[pl.BlockSpec((1, tc, LANES), lambda bi, ti: (bi, ti, 0)), pair_major, pair_major,
                  _const_spec(place.shape), _const_spec(sel.shape)],
        out_specs=(pl.BlockSpec((1, N_HEADS, tc, LANES), lambda bi, ti: (bi, 0, ti, 0)),
                   pl.BlockSpec((1, 1, STAT_ROWS, LANES), lambda bi, ti: (bi, ti, 0, 0))),
        out_shape=(jax.ShapeDtypeStruct((b, N_HEADS, t, LANES), BF16),
                   jax.ShapeDtypeStruct((b, t // tc, STAT_ROWS, LANES), F32)),
        scratch_shapes=[pltpu.VMEM((1, LANES), F32)],
        compiler_params=pltpu.CompilerParams(dimension_semantics=("arbitrary", "arbitrary")),
        name="attn_prep",
    )(logf, kb, q, place, sel)


def _cumsum_kernel(lf_ref, off_ref, c_ref, ct_ref, carry):
    @pl.when(pl.program_id(1) == 0)
    def _():
        carry[...] = off_ref[0]

    n = lf_ref.shape[1]
    c = _cumsum_rows(lf_ref[0]) + carry[...]
    carry[...] = c[n - 1:n, :]
    c_ref[0] = c
    ct_ref[0] = c.T


def _cumsum_heads(logf, offset):
    b, t, _ = logf.shape
    tc = min(ATTN_BLOCK, t)
    return pl.pallas_call(
        _cumsum_kernel,
        grid=(b, t // tc),
        in_specs=[pl.BlockSpec((1, tc, LANES), lambda bi, ti: (bi, ti, 0)),
                  pl.BlockSpec((1, 1, LANES), lambda bi, ti: (bi, 0, 0))],
        out_specs=(pl.BlockSpec((1, tc, LANES), lambda bi, ti: (bi, ti, 0)),
                   pl.BlockSpec((1, LANES, tc), lambda bi, ti: (bi, 0, ti))),
        out_shape=(jax.ShapeDtypeStruct((b, t, LANES), F32), jax.ShapeDtypeStruct((b, LANES, t), F32)),
        scratch_shapes=[pltpu.VMEM((1, LANES), F32)],
        compiler_params=pltpu.CompilerParams(dimension_semantics=("arbitrary", "arbitrary")),
        name="cumsum_heads",
    )(logf, offset)


def _prompt_attn_kernel(st_ref, q_ref, ka_ref, vt_ref, ones_ref, o_ref, qa_ref, m_ref, l_ref, acc_ref):
    bi, j, r = pl.program_id(0), pl.program_id(1), pl.program_id(2)
    nblk = pl.num_programs(1)
    kb = jnp.maximum(j - r, 0)
    tq, tk = q_ref.shape[2], ka_ref.shape[2]
    stat = lambda blk, which, h: st_ref[((bi * nblk + blk) * N_STATS + which) * N_HEADS + h]

    @pl.when(r == 0)
    def _():
        m_ref[...] = jnp.full(m_ref.shape, -jnp.inf, F32)
        l_ref[...] = jnp.zeros(l_ref.shape, F32)
        acc_ref[...] = jnp.zeros(acc_ref.shape, F32)
        lane = lax.broadcasted_iota(jnp.int32, (tq, LANES), 1)
        for h in range(N_HEADS):
            qa_ref[h] = jnp.where(_own_lanes(lane, h), q_ref[0, h // 2], ones_ref[...])

    def scores(h):
        return lax.dot_general(ka_ref[0, h], qa_ref[h], (((1,), (1,)), ((), ())), preferred_element_type=F32)

    def pair_step(hp, masked):
        if masked:
            allowed = (lax.broadcasted_iota(jnp.int32, (tk, tq), 0) <= lax.broadcasted_iota(jnp.int32, (tk, tq), 1))
        both = [scores(2 * hp), scores(2 * hp + 1)]
        for e in range(2):
            h = 2 * hp + e
            s = jnp.where(allowed, both[e], -jnp.inf) if masked else both[e]
            base = stat(kb, STAT_BASE, h)
            m_old = m_ref[h:h + 1, :]
            m_new = jnp.maximum(m_old, jnp.max(s, axis=0, keepdims=True) - base)
            alpha = jnp.exp2(m_old - m_new)
            p = jnp.exp2(s - (m_new + base))
            l_ref[h:h + 1, :] = alpha * l_ref[h:h + 1, :] + jnp.sum(p, axis=0, keepdims=True)
            if e == 0:
                alpha = alpha + _zero_from(both[1][tk - 8:tk, tq - LANES:tq])[0:1, 0:1]
            hs = slice(e * HEAD_DIM, (e + 1) * HEAD_DIM)
            acc_ref[hp, hs, :] = alpha * acc_ref[hp, hs, :] + jnp.dot(
                vt_ref[0, hp, hs, :], p.astype(BF16), preferred_element_type=F32)
            m_ref[h:h + 1, :] = m_new

    @pl.when(r == 0)
    def _():
        for hp in range(N_PAIRS):
            pair_step(hp, True)

    @pl.when(jnp.logical_and(r > 0, r <= j))
    def _():
        live = []
        for h in range(N_HEADS):
            bmax = stat(kb, STAT_BMAX, h)
            bound = (stat(j, STAT_QNORM, h) * stat(kb, STAT_KNORM, h) + bmax
                     + jnp.abs(bmax) * 2.0 ** -20 + SKIP_MARGIN)
            live.append(bound - jnp.min(m_ref[h:h + 1, :]) > EXP2_ZERO_BELOW)
        for hp in range(N_PAIRS):
            @pl.when(jnp.logical_or(live[2 * hp], live[2 * hp + 1]))
            def _():
                pair_step(hp, False)

    @pl.when(r == j)
    def _():
        inv = 1.0 / l_ref[...]
        for hp in range(N_PAIRS):
            scale = jnp.concatenate([jnp.broadcast_to(inv[2 * hp + e:2 * hp + e + 1, :], (HEAD_DIM, tq))
                                     for e in range(2)], axis=0)
            o_ref[0, :, hp * LANES:(hp + 1) * LANES] = (acc_ref[hp] * scale).T.astype(BF16)


def _prompt_attention(q, ka, vt, stats):
    b, _, t, _ = q.shape
    tq = min(ATTN_BLOCK, t)
    nq = t // tq
    ones = _query_ones()[:tq]
    key_block = lambda j, r: jnp.maximum(j - r, 0)
    grid_spec = pltpu.PrefetchScalarGridSpec(
        num_scalar_prefetch=1, grid=(b, nq, nq),
        in_specs=[pl.BlockSpec((1, N_PAIRS, tq, LANES), lambda bi, j, r, *_: (bi, 0, j, 0)),
                  pl.BlockSpec((1, N_HEADS, tq, LANES), lambda bi, j, r, *_: (bi, 0, key_block(j, r), 0)),
                  pl.BlockSpec((1, N_PAIRS, LANES, tq), lambda bi, j, r, *_: (bi, 0, 0, key_block(j, r))),
                  pl.BlockSpec((tq, LANES), lambda bi, j, r, *_: (0, 0))],
        out_specs=pl.BlockSpec((1, tq, N_PAIRS * LANES), lambda bi, j, r, *_: (bi, j, 0)),
        scratch_shapes=[pltpu.VMEM((N_HEADS, tq, LANES), BF16),
                        pltpu.VMEM((N_HEADS, tq), F32), pltpu.VMEM((N_HEADS, tq), F32),
                        pltpu.VMEM((N_PAIRS, LANES, tq), F32)])
    return pl.pallas_call(
        _prompt_attn_kernel, grid_spec=grid_spec,
        out_shape=jax.ShapeDtypeStruct((b, t, N_PAIRS * LANES), BF16),
        compiler_params=pltpu.CompilerParams(
            dimension_semantics=("parallel", "parallel", "arbitrary"), vmem_limit_bytes=VMEM_LIMIT_BYTES),
        name="prompt_attention",
    )(stats[:, :, :N_STATS, :N_HEADS].reshape(-1), q, ka, vt, ones)


def _sample_attn_kernel(q_ref, pk_ref, pv_ref, kn_ref, vn_ref, cpt_ref, cnt_ref, o_ref,
                        qw_ref, m_ref, l_ref, acc_ref, *, n_past):
    i = pl.program_id(1)
    tq = q_ref.shape[2]
    rows = 2 * tq

    @pl.when(i == 0)
    def _():
        m_ref[...] = jnp.full(m_ref.shape, -jnp.inf, F32)
        l_ref[...] = jnp.zeros(l_ref.shape, F32)
        acc_ref[...] = jnp.zeros(acc_ref.shape, F32)
        lane = lax.broadcasted_iota(jnp.int32, (tq, LANES), 1)
        for hp in range(N_PAIRS):
            pair = q_ref[0, hp]
            for e in range(2):
                qw_ref[hp, e * tq:(e + 1) * tq, :] = jnp.where(_own_lanes(lane, e), pair, jnp.zeros_like(pair))

    def step(k_of, v_of, ckt, masked):
        tk = ckt.shape[1]
        s = jnp.concatenate(
            [lax.dot_general(qw_ref[hp], k_of(hp), (((1,), (1,)), ((), ())), preferred_element_type=F32)
             for hp in range(N_PAIRS)], axis=0)
        ck = jnp.concatenate([jnp.broadcast_to(ckt[h:h + 1, :], (tq, tk)) for h in range(N_HEADS)], axis=0)
        s = s - ck * LOG2E
        if masked:
            q_idx = lax.rem(lax.broadcasted_iota(jnp.int32, s.shape, 0), tq)
            s = jnp.where(lax.broadcasted_iota(jnp.int32, s.shape, 1) <= q_idx, s, -jnp.inf)
        m_old = m_ref[...]
        m_new = jnp.maximum(m_old, jnp.max(s, axis=1, keepdims=True))
        alpha = jnp.exp2(m_old - m_new)
        p = jnp.exp2(s - m_new)
        l_ref[...] = alpha * l_ref[...] + jnp.sum(p, axis=1, keepdims=True)
        m_ref[...] = m_new
        pb = p.astype(BF16)
        for hp in range(N_PAIRS):
            rs = slice(hp * rows, (hp + 1) * rows)
            acc_ref[hp] = alpha[rs, :] * acc_ref[hp] + jnp.dot(pb[rs, :], v_of(hp), preferred_element_type=F32)

    pair_lanes = lambda hp: slice(hp * LANES, (hp + 1) * LANES)

    @pl.when(i < n_past)
    def _():
        step(lambda hp: pk_ref[0, :, pair_lanes(hp)].astype(BF16),
             lambda hp: pv_ref[0, :, pair_lanes(hp)].astype(BF16), cpt_ref[0], False)

    @pl.when(i == n_past)
    def _():
        step(lambda hp: kn_ref[0, hp], lambda hp: vn_ref[0, hp], cnt_ref[0], True)
        inv = 1.0 / l_ref[...]
        for hp in range(N_PAIRS):
            o = acc_ref[hp] * inv[hp * rows:(hp + 1) * rows, :]
            for e in range(2):
                lo = hp * LANES + e * HEAD_DIM
                o_ref[0, :, lo:lo + HEAD_DIM] = o[e * tq:(e + 1) * tq, e * HEAD_DIM:(e + 1) * HEAD_DIM].astype(BF16)


def _sample_attention(q, past_k, past_v, kn, vn, cpt, cnt):
    b, _, tq, _ = q.shape
    p_len, a_dim = past_k.shape[1], past_k.shape[2]
    tk = min(ATTN_BLOCK, p_len)
    n_past = p_len // tk
    past = lambda bi, i: (bi, jnp.minimum(i, n_past - 1), 0)
    whole4 = pl.BlockSpec((1, N_PAIRS, tq, LANES), lambda bi, i: (bi, 0, 0, 0))
    return pl.pallas_call(
        functools.partial(_sample_attn_kernel, n_past=n_past),
        grid=(b, n_past + 1),
        in_specs=[whole4,
                  pl.BlockSpec((1, tk, a_dim), past), pl.BlockSpec((1, tk, a_dim), past),
                  whole4, whole4,
                  pl.BlockSpec((1, N_HEADS, tk), lambda bi, i: (bi, 0, jnp.minimum(i, n_past - 1))),
                  pl.BlockSpec((1, N_HEADS, tq), lambda bi, i: (bi, 0, 0))],
        out_specs=pl.BlockSpec((1, tq, a_dim), lambda bi, i: (bi, 0, 0)),
        out_shape=jax.ShapeDtypeStruct((b, tq, a_dim), BF16),
        scratch_shapes=[pltpu.VMEM((N_PAIRS, 2 * tq, LANES), BF16),
                        pltpu.VMEM((N_HEADS * tq, 1), F32), pltpu.VMEM((N_HEADS * tq, 1), F32),
                        pltpu.VMEM((N_PAIRS, 2 * tq, LANES), F32)],
        compiler_params=pltpu.CompilerParams(
            dimension_semantics=("parallel", "arbitrary"), vmem_limit_bytes=VMEM_LIMIT_BYTES),
        name="sample_attention",
    )(q, past_k, past_v, kn, vn, cpt, cnt)


def _merge_ffn_kernel(x_ref, ylru_ref, yatt_ref, nmix_ref, wg_ref, bgate_ref, wbl_ref, wba_ref, wout_ref,
                      nffn_ref, wfi_ref, wfo_ref, nfin_ref, y_ref):
    d = x_ref.shape[1]
    d_ff = wfo_ref.shape[0]
    x = x_ref[...]
    xb = _rmsnorm(x, nmix_ref[...]).astype(BF16)
    g = jax.nn.sigmoid(jnp.dot(xb, wg_ref[...], preferred_element_type=F32) + bgate_ref[...])
    mixed = (g[:, :d] * jnp.dot(ylru_ref[...], wbl_ref[...], preferred_element_type=F32)
             + g[:, d:] * jnp.dot(yatt_ref[...], wba_ref[...], preferred_element_type=F32))
    x = x + jnp.dot(mixed.astype(BF16), wout_ref[...], preferred_element_type=F32)
    xb2 = _rmsnorm(x, nffn_ref[...]).astype(BF16)
    hid = jnp.dot(xb2, wfi_ref[...], preferred_element_type=F32)
    gf, up = hid[:, :d_ff], hid[:, d_ff:]
    act = (gf * jax.nn.sigmoid(gf) * up).astype(BF16)
    x = x + jnp.dot(act, wfo_ref[...], preferred_element_type=F32)
    y_ref[...] = _rmsnorm(x, nfin_ref[...])


def _merge_ffn(x, ylru, yatt, w, norm_final):
    n, d = x.shape
    tb = min(TOKEN_BLOCK, n)
    tok = pl.BlockSpec((tb, d), lambda i: (i, 0))
    consts = [w["norm_mix"], w["w_g"], w["b_gate"], w["w_br_lru"], w["w_br_att"], w["w_out"],
              w["norm_ffn"], w["w_ffn_in"], w["w_ffn_out"], norm_final]
    return pl.pallas_call(
        _merge_ffn_kernel,
        grid=(n // tb,),
        in_specs=[tok, tok, tok] + [_const_spec(c.shape) for c in consts],
        out_specs=tok,
        out_shape=jax.ShapeDtypeStruct((n, d), F32),
        compiler_params=pltpu.CompilerParams(
            dimension_semantics=("parallel",), vmem_limit_bytes=VMEM_LIMIT_BYTES),
        name="merge_ffn",
    )(x, ylru, yatt, *consts)


def _prep_layer_weights(norm_mix, w_in, b_forget, b_gate, conv_w, conv_b, w_rg, b_rg, w_ig, b_ig, lam,
                        w_br_lru, w_br_att, w_out, norm_ffn, w_ffn_in, w_ffn_out):
    d = w_in.shape[0]
    a_dim = N_HEADS * HEAD_DIM
    n1 = 2 * d + 3 * a_dim
    row = lambda v: v.reshape(1, -1).astype(F32)
    w_f = jnp.pad(w_in[:, n1 + 2 * d:], ((0, 0), (0, LANES - N_HEADS)))
    per_group = MXU_DIM // (d // N_LRU_BLOCKS)

    def block_diag(wb):
        g = wb.reshape(-1, per_group, wb.shape[1], wb.shape[2])
        eye = jnp.eye(per_group, dtype=wb.dtype)
        return jnp.einsum("gaij,ab->gaibj", g, eye).reshape(g.shape[0], MXU_DIM, MXU_DIM)

    return {
        "norm_mix": row(norm_mix),
        "w_in1": jnp.concatenate([w_in[:, :n1], w_f], axis=1).astype(BF16),
        "w_g": w_in[:, n1:n1 + 2 * d].astype(BF16),
        "b_forget": jnp.pad(row(b_forget), ((0, 0), (0, LANES - N_HEADS))),
        "b_gate": row(b_gate),
        "conv_w": conv_w.astype(F32), "conv_b": row(conv_b),
        "w_gates": jnp.concatenate([block_diag(w_rg), block_diag(w_ig)], axis=2).astype(BF16),
        "b_rg": row(b_rg), "b_ig": row(b_ig), "lam": row(lam),
        "w_br_lru": w_br_lru.astype(BF16), "w_br_att": w_br_att.astype(BF16), "w_out": w_out.astype(BF16),
        "norm_ffn": row(norm_ffn), "w_ffn_in": w_ffn_in.astype(BF16), "w_ffn_out": w_ffn_out.astype(BF16),
    }


def _layer_prompt(x, w, norm_final):
    b, t, d = x.shape
    zeros_conv = jnp.zeros((b, CONV_WIDTH - 1, d), F32)
    zeros_h = jnp.zeros((b, 1, d), F32)
    q, kb, _, vt, kf, vf, logf, ylru, conv_o, h_last = _proj_lru(x, zeros_conv, zeros_h, w, reset_first=True)
    ka, stats = _attn_prep(logf, kb, q)
    yatt = _prompt_attention(q, ka, vt, stats)
    y = _merge_ffn(x.reshape(b * t, d), ylru.reshape(b * t, d), yatt.reshape(b * t, d), w, norm_final)
    return y.reshape(b, t, d), kf, vf, logf[..., :N_HEADS], conv_o, h_last[:, 0]


def _layer_sample(x, conv0, h0, past_k, past_v, past_logf, w, norm_final):
    b, t, d = x.shape
    p_len = past_k.shape[1]
    a_dim = N_HEADS * HEAD_DIM
    q, kb, vb, _, kf, vf, logf, ylru, conv_o, h_last = _proj_lru(x, conv0, h0[:, None, :], w, reset_first=False)
    past_lf = jnp.pad(past_logf.astype(F32), ((0, 0), (0, 0), (0, LANES - N_HEADS)))
    c_past, ct_past = _cumsum_heads(past_lf, jnp.zeros((b, 1, LANES), F32))
    _, ct_new = _cumsum_heads(logf, c_past[:, p_len - 1:p_len, :])
    yatt = _sample_attention(q, past_k.reshape(b, p_len, a_dim), past_v.reshape(b, p_len, a_dim), kb, vb,
                             ct_past, ct_new)
    y = _merge_ffn(x.reshape(b * t, d), ylru.reshape(b * t, d), yatt.reshape(b * t, d), w, norm_final)
    return y.reshape(b, t, d), kf, vf, logf[..., :N_HEADS], conv_o, h_last[:, 0]


def kernel(x_prompt, x_sample, cache_k, cache_v, cache_logf, state_conv, state_h, norm_mix, w_in, b_forget,
           b_gate, conv_w, conv_b, w_rg, b_rg, w_ig, b_ig, lru_lambda, w_br_lru, w_br_att, w_out, norm_ffn,
           w_ffn_in, w_ffn_out, norm_final):
    assert norm_mix.shape[0] == 1, "single-layer trunk: the final rmsnorm is fused into the layer's last stage"
    heads = lambda kv: kv.reshape(kv.shape[0], kv.shape[1], N_HEADS, HEAD_DIM)[None]
    nfin = norm_final.reshape(1, -1).astype(F32)
    w = _prep_layer_weights(norm_mix[0], w_in[0], b_forget[0], b_gate[0], conv_w[0], conv_b[0], w_rg[0], b_rg[0],
                            w_ig[0], b_ig[0], lru_lambda[0], w_br_lru[0], w_br_att[0], w_out[0], norm_ffn[0],
                            w_ffn_in[0], w_ffn_out[0])
    yp, kp, vp, lfp, cp, hp = _layer_prompt(x_prompt, w, nfin)
    ys, ks, vs, lfs, cs, hs = _layer_sample(x_sample, state_conv[0], state_h[0], cache_k[0], cache_v[0],
                                            cache_logf[0], w, nfin)
    return (yp, ys, heads(kp), heads(vp), lfp[None], cp[None], hp[None],
            heads(ks), heads(vs), lfs[None], cs[None], hs[None])
```

```python
import functools
import math

import numpy as np
import jax
import jax.numpy as jnp
from jax import lax
from jax.experimental import pallas as pl
from jax.experimental.pallas import tpu as pltpu

N_HEADS = 16
HEAD_DIM = 64
N_LRU_BLOCKS = 16
CONV_WIDTH = 4
LRU_C = 8.0
EPS = 1e-6

LANES = 128
MXU_DIM = 256
VMEM_LIMIT_BYTES = 56 * 1024 * 1024
CARRY_ROWS = 8

N_PAIRS = N_HEADS // 2
LOG2E = math.log2(math.e)
N_SPLIT = 3
ATTN_BLOCK = 512
TOKEN_BLOCK = 256
SAMPLE_KEY_BLOCK = 1024

STAT_BASE, STAT_BMAX, STAT_KNORM, STAT_QNORM, N_STATS = 0, 1, 2, 3, 4
STAT_ROWS = 8
NORM_SLACK = 1.01
EXP2_ZERO_BELOW = -150.0
SKIP_MARGIN = 4.0

F32 = jnp.float32
BF16 = jnp.bfloat16
U32 = jnp.uint32


def _log_sigmoid(x):
    return jnp.minimum(x, 0.0) - jnp.log1p(jnp.exp(-jnp.abs(x)))


def _gelu_tanh(x):
    c = math.sqrt(2.0 / math.pi)
    return 0.5 * x * (1.0 + jnp.tanh(c * (x + 0.044715 * (x * x * x))))


def _rmsnorm(x, g):
    return x * lax.rsqrt(jnp.mean(x * x, axis=-1, keepdims=True) + EPS) * g


def _scan_affine_rows(a, b):
    n = a.shape[0]
    row = lax.broadcasted_iota(jnp.int32, a.shape, 0)
    d = 1
    while d < n:
        keep = row >= d
        a_sh = jnp.where(keep, pltpu.roll(a, d, 0), 1.0)
        b_sh = jnp.where(keep, pltpu.roll(b, d, 0), 0.0)
        b = a * b_sh + b
        a = a * a_sh
        d *= 2
    return a, b


def _cumsum_rows(x):
    n = x.shape[0]
    row = lax.broadcasted_iota(jnp.int32, x.shape, 0)
    d = 1
    while d < n:
        x = x + jnp.where(row >= d, pltpu.roll(x, d, 0), 0.0)
        d *= 2
    return x


def _const_spec(shape):
    return pl.BlockSpec(shape, lambda *_: (0,) * len(shape), pipeline_mode=pl.Buffered(1))


def _own_lanes(lane, head):
    return (lane < HEAD_DIM) if head % 2 == 0 else (lane >= HEAD_DIM)


def _aug_lane0(head):
    return HEAD_DIM if head % 2 == 0 else 0


def _zero_from(x):
    u = pltpu.bitcast(x, U32)
    return pltpu.bitcast(lax.shift_right_logical(lax.shift_right_logical(u, U32(16)), U32(16)), F32)


def _proj_lru_kernel(x_ref, conv0_ref, h0_ref, nmix_ref, win_ref, convw_ref, convb_ref, wgate_ref,
                     brg_ref, big_ref, lam_ref, bf_ref,
                     q_ref, kb_ref, vb_ref, vt_ref, kf_ref, vf_ref, logf_ref, ylru_ref, convo_ref, hlast_ref,
                     xr_buf, h_carry, *, reset_first):
    t = pl.program_id(1)
    tb, d = x_ref.shape[1], x_ref.shape[2]
    a_dim = kf_ref.shape[2]

    xb = _rmsnorm(x_ref[0], nmix_ref[...]).astype(BF16)

    def proj(lo, width):
        return jnp.dot(xb, win_ref[:, lo:lo + width], preferred_element_type=F32)

    xr = proj(0, d)
    gate = proj(d, d)
    q = proj(2 * d, a_dim)
    k = proj(2 * d + a_dim, a_dim)
    v = proj(2 * d + 2 * a_dim, a_dim)
    f_logit = proj(2 * d + 3 * a_dim, LANES)

    qb = (q * (LOG2E * HEAD_DIM ** -0.5)).astype(BF16)
    kb, vb, vtb = k.astype(BF16), v.astype(BF16), v.T.astype(BF16)
    for hp in range(N_PAIRS):
        ls = slice(hp * LANES, (hp + 1) * LANES)
        q_ref[0, hp] = qb[:, ls]
        kb_ref[0, hp] = kb[:, ls]
        vb_ref[0, hp] = vb[:, ls]
    for h in range(N_HEADS):
        vt_ref[0, h] = vtb[h * HEAD_DIM:(h + 1) * HEAD_DIM, :]
    kf_ref[0] = k
    vf_ref[0] = v
    logf_ref[0] = _log_sigmoid(f_logit + bf_ref[...])

    lo = CARRY_ROWS - (CONV_WIDTH - 1)

    @pl.when(t == 0)
    def _():
        xr_buf[lo:CARRY_ROWS, :] = conv0_ref[0]
        h_carry[...] = h0_ref[0]

    xr_buf[CARRY_ROWS:CARRY_ROWS + tb, :] = xr
    xc = convb_ref[...] + xr_buf[lo:lo + tb, :] * convw_ref[0:1, :]
    for j in range(1, CONV_WIDTH):
        xc = xc + xr_buf[lo + j:lo + j + tb, :] * convw_ref[j:j + 1, :]
    tail = xr_buf[tb + lo:tb + CARRY_ROWS, :]
    convo_ref[0] = tail
    xr_buf[lo:CARRY_ROWS, :] = tail

    xcb = xc.astype(BF16)
    parts = [jnp.dot(xcb[:, g * MXU_DIM:(g + 1) * MXU_DIM], wgate_ref[g], preferred_element_type=F32)
             for g in range(d // MXU_DIM)]
    r = jax.nn.sigmoid(jnp.concatenate([p[:, :MXU_DIM] for p in parts], axis=1) + brg_ref[...])
    i = jax.nn.sigmoid(jnp.concatenate([p[:, MXU_DIM:] for p in parts], axis=1) + big_ref[...])
    a = jnp.exp(LRU_C * r * _log_sigmoid(lam_ref[...]))
    mult = jnp.sqrt(1.0 - a * a)
    if reset_first:
        row = lax.broadcasted_iota(jnp.int32, mult.shape, 0)
        mult = jnp.where(jnp.logical_and(row == 0, t == 0), 1.0, mult)
    a_cum, b_cum = _scan_affine_rows(a, mult * i * xc)
    h = a_cum * h_carry[...] + b_cum
    h_carry[...] = h[tb - 1:tb, :]
    hlast_ref[0] = h[tb - 1:tb, :]
    ylru_ref[0] = (h * _gelu_tanh(gate)).astype(BF16)


def _proj_lru(x, conv0, h0, w, *, reset_first):
    b, t, d = x.shape
    a_dim = N_HEADS * HEAD_DIM
    tb = min(TOKEN_BLOCK, t)
    grid = (b, t // tb)
    tok = lambda width: pl.BlockSpec((1, tb, width), lambda bi, ti: (bi, ti, 0))
    per_b = lambda rows: pl.BlockSpec((1, rows, d), lambda bi, ti: (bi, 0, 0))
    pair_major = pl.BlockSpec((1, N_PAIRS, tb, LANES), lambda bi, ti: (bi, 0, ti, 0))
    out_shapes = (
        jax.ShapeDtypeStruct((b, N_PAIRS, t, LANES), BF16),
        jax.ShapeDtypeStruct((b, N_PAIRS, t, LANES), BF16),
        jax.ShapeDtypeStruct((b, N_PAIRS, t, LANES), BF16),
        jax.ShapeDtypeStruct((b, N_HEADS, HEAD_DIM, t), BF16),
        jax.ShapeDtypeStruct((b, t, a_dim), F32),
        jax.ShapeDtypeStruct((b, t, a_dim), F32),
        jax.ShapeDtypeStruct((b, t, LANES), F32),
        jax.ShapeDtypeStruct((b, t, d), BF16),
        jax.ShapeDtypeStruct((b, CONV_WIDTH - 1, d), F32),
        jax.ShapeDtypeStruct((b, 1, d), F32),
    )
    out_specs = (
        pair_major, pair_major, pair_major,
        pl.BlockSpec((1, N_HEADS, HEAD_DIM, tb), lambda bi, ti: (bi, 0, 0, ti)),
        tok(a_dim), tok(a_dim), tok(LANES), tok(d),
        per_b(CONV_WIDTH - 1), per_b(1),
    )
    in_specs = [
        tok(d), per_b(CONV_WIDTH - 1), per_b(1),
        _const_spec(w["norm_mix"].shape), _const_spec(w["w_in1"].shape),
        _const_spec(w["conv_w"].shape), _const_spec(w["conv_b"].shape), _const_spec(w["w_gates"].shape),
        _const_spec(w["b_rg"].shape), _const_spec(w["b_ig"].shape), _const_spec(w["lam"].shape),
        _const_spec(w["b_forget"].shape),
    ]
    return pl.pallas_call(
        functools.partial(_proj_lru_kernel, reset_first=reset_first),
        grid=grid, in_specs=in_specs, out_specs=out_specs, out_shape=out_shapes,
        scratch_shapes=[pltpu.VMEM((CARRY_ROWS + tb, d), F32), pltpu.VMEM((1, d), F32)],
        compiler_params=pltpu.CompilerParams(
            dimension_semantics=("arbitrary", "arbitrary"), vmem_limit_bytes=VMEM_LIMIT_BYTES),
        name="proj_lru",
    )(x, conv0, h0, w["norm_mix"], w["w_in1"], w["conv_w"], w["conv_b"], w["w_gates"],
      w["b_rg"], w["b_ig"], w["lam"], w["b_forget"])


def _bias_placement():
    place = np.zeros((N_SPLIT * LANES, N_PAIRS * LANES), np.float32)
    for h in range(N_HEADS):
        for s in range(N_SPLIT):
            place[s * LANES + h, (h // 2) * LANES + _aug_lane0(h) + s] = 1.0
    return jnp.asarray(place, BF16)


def _head_lane_sum():
    sel = np.zeros((N_PAIRS * LANES, LANES), np.float32)
    for h in range(N_HEADS):
        lo = (h // 2) * LANES + (h % 2) * HEAD_DIM
        sel[lo:lo + HEAD_DIM, h] = 1.0
    return jnp.asarray(sel, BF16)


def _query_ones():
    ones = np.zeros((ATTN_BLOCK, LANES), np.float32)
    for h in range(2):
        ones[:, _aug_lane0(h):_aug_lane0(h) + N_SPLIT] = 1.0
    return jnp.asarray(ones, BF16)


def _max_head_norm(slabs_ref, sel_ref):
    sq = jnp.concatenate([jnp.square(slabs_ref[0, hp].astype(F32)) for hp in range(N_PAIRS)], axis=1)
    norm2 = jnp.dot(sq.astype(BF16), sel_ref[...], preferred_element_type=F32)
    return jnp.sqrt(jnp.max(norm2, axis=0, keepdims=True) * NORM_SLACK)


def _attn_prep_kernel(lf_ref, k_ref, q_ref, place_ref, sel_ref, ka_ref, st_ref, carry):
    @pl.when(pl.program_id(1) == 0)
    def _():
        carry[...] = jnp.zeros(carry.shape, F32)

    n = lf_ref.shape[1]
    c = _cumsum_rows(lf_ref[0]) + carry[...]
    carry[...] = c[n - 1:n, :]
    base = c[0:1, :]
    rest = (base - c) * LOG2E
    pieces = []
    for _ in range(N_SPLIT):
        piece = rest.astype(BF16)
        pieces.append(piece)
        rest = rest - piece.astype(F32)
    placed = jnp.dot(jnp.concatenate(pieces, axis=1), place_ref[...], preferred_element_type=F32)
    lane = lax.broadcasted_iota(jnp.int32, (n, LANES), 1)
    for h in range(N_HEADS):
        bias = placed[:, (h // 2) * LANES:(h // 2 + 1) * LANES].astype(BF16)
        ka_ref[0, h] = jnp.where(_own_lanes(lane, h), k_ref[0, h // 2], bias)

    stats = {STAT_BASE: base * LOG2E,
             STAT_BMAX: c[n - 1:n, :] * -LOG2E,
             STAT_KNORM: _max_head_norm(k_ref, sel_ref),
             STAT_QNORM: _max_head_norm(q_ref, sel_ref)}
    rows = [stats.get(r, jnp.zeros((1, LANES), F32)) for r in range(STAT_ROWS)]
    st_ref[0, 0] = jnp.concatenate(rows, axis=0)


def _attn_prep(logf, kb, q):
    b, t, _ = logf.shape
    tc = min(ATTN_BLOCK, t)
    place, sel = _bias_placement(), _head_lane_sum()
    pair_major = pl.BlockSpec((1, N_PAIRS, tc, LANES), lambda bi, ti: (bi, 0, ti, 0))
    return pl.pallas_call(
        _attn_prep_kernel,
        grid=(b, t // tc),
        in_specs=[pl.BlockSpec((1, tc, LANES), lambda bi, ti: (bi, ti, 0)), pair_major, pair_major,
                  _const_spec(place.shape), _const_spec(sel.shape)],
        out_specs=(pl.BlockSpec((1, N_HEADS, tc, LANES), lambda bi, ti: (bi, 0, ti, 0)),
                   pl.BlockSpec((1, 1, STAT_ROWS, LANES), lambda bi, ti: (bi, ti, 0, 0))),
        out_shape=(jax.ShapeDtypeStruct((b, N_HEADS, t, LANES), BF16),
                   jax.ShapeDtypeStruct((b, t // tc, STAT_ROWS, LANES), F32)),
        scratch_shapes=[pltpu.VMEM((1, LANES), F32)],
        compiler_params=pltpu.CompilerParams(dimension_semantics=("arbitrary", "arbitrary")),
        name="attn_prep",
    )(logf, kb, q, place, sel)


def _cumsum_kernel(lf_ref, off_ref, c_ref, ct_ref, carry):
    @pl.when(pl.program_id(1) == 0)
    def _():
        carry[...] = off_ref[0]

    n = lf_ref.shape[1]
    c = _cumsum_rows(lf_ref[0]) + carry[...]
    carry[...] = c[n - 1:n, :]
    c_ref[0] = c
    ct_ref[0] = c.T


def _cumsum_heads(logf, offset):
    b, t, _ = logf.shape
    tc = min(ATTN_BLOCK, t)
    return pl.pallas_call(
        _cumsum_kernel,
        grid=(b, t // tc),
        in_specs=[pl.BlockSpec((1, tc, LANES), lambda bi, ti: (bi, ti, 0)),
                  pl.BlockSpec((1, 1, LANES), lambda bi, ti: (bi, 0, 0))],
        out_specs=(pl.BlockSpec((1, tc, LANES), lambda bi, ti: (bi, ti, 0)),
                   pl.BlockSpec((1, LANES, tc), lambda bi, ti: (bi, 0, ti))),
        out_shape=(jax.ShapeDtypeStruct((b, t, LANES), F32), jax.ShapeDtypeStruct((b, LANES, t), F32)),
        scratch_shapes=[pltpu.VMEM((1, LANES), F32)],
        compiler_params=pltpu.CompilerParams(dimension_semantics=("arbitrary", "arbitrary")),
        name="cumsum_heads",
    )(logf, offset)


def _prompt_attn_kernel(st_ref, order_ref, q_ref, ka_ref, vt_ref, ones_ref, o_ref, qa_ref, m_ref, l_ref, acc_ref):
    bi, j, r = pl.program_id(0), pl.program_id(1), pl.program_id(2)
    nblk = pl.num_programs(1)
    kb = jnp.maximum(j - r, 0)
    tq, tk = q_ref.shape[2], ka_ref.shape[2]
    stat = lambda blk, which, h: st_ref[((bi * nblk + blk) * N_STATS + which) * N_HEADS + h]

    @pl.when(r == 0)
    def _():
        m_ref[...] = jnp.full(m_ref.shape, -jnp.inf, F32)
        l_ref[...] = jnp.zeros(l_ref.shape, F32)
        acc_ref[...] = jnp.zeros(acc_ref.shape, F32)
        lane = lax.broadcasted_iota(jnp.int32, (tq, LANES), 1)
        for h in range(N_HEADS):
            qa_ref[h] = jnp.where(_own_lanes(lane, h), q_ref[0, h // 2], ones_ref[...])

    def scores(h):
        return lax.dot_general(ka_ref[0, h], qa_ref[h], (((1,), (1,)), ((), ())), preferred_element_type=F32)

    def pair_step(heads, masked):
        if masked:
            allowed = (lax.broadcasted_iota(jnp.int32, (tk, tq), 0) <= lax.broadcasted_iota(jnp.int32, (tk, tq), 1))
        both = [scores(h) for h in heads]
        for e, h in enumerate(heads):
            s = jnp.where(allowed, both[e], -jnp.inf) if masked else both[e]
            base = stat(kb, STAT_BASE, h)
            m_old = m_ref[pl.ds(h, 1), :]
            m_new = jnp.maximum(m_old, jnp.max(s, axis=0, keepdims=True) - base)
            alpha = jnp.exp2(m_old - m_new)
            p = jnp.exp2(s - (m_new + base))
            l_ref[pl.ds(h, 1), :] = alpha * l_ref[pl.ds(h, 1), :] + jnp.sum(p, axis=0, keepdims=True)
            if e == 0:
                alpha = alpha + _zero_from(both[1][tk - 8:tk, tq - LANES:tq])[0:1, 0:1]
            acc_ref[h] = alpha * acc_ref[h] + jnp.dot(vt_ref[0, h], p.astype(BF16), preferred_element_type=F32)
            m_ref[pl.ds(h, 1), :] = m_new

    def is_live(h):
        bmax = stat(kb, STAT_BMAX, h)
        bound = (stat(j, STAT_QNORM, h) * stat(kb, STAT_KNORM, h) + bmax
                 + jnp.abs(bmax) * 2.0 ** -20 + SKIP_MARGIN)
        return bound - jnp.min(m_ref[pl.ds(h, 1), :]) > EXP2_ZERO_BELOW

    @pl.when(r == 0)
    def _():
        for hp in range(N_PAIRS):
            pair_step((2 * hp, 2 * hp + 1), True)

    @pl.when(jnp.logical_and(r > 0, r <= j))
    def _():
        for slot in range(N_PAIRS):
            heads = tuple(order_ref[bi * N_HEADS + 2 * slot + e] for e in range(2))

            @pl.when(jnp.logical_or(is_live(heads[0]), is_live(heads[1])))
            def _():
                pair_step(heads, False)

    @pl.when(r == j)
    def _():
        inv = 1.0 / l_ref[...]
        for hp in range(N_PAIRS):
            o = jnp.concatenate([acc_ref[2 * hp + e] * inv[2 * hp + e:2 * hp + e + 1, :] for e in range(2)], axis=0)
            o_ref[0, :, hp * LANES:(hp + 1) * LANES] = o.T.astype(BF16)


def _prompt_attention(q, ka, vt, stats):
    b, _, t, _ = q.shape
    tq = min(ATTN_BLOCK, t)
    nq = t // tq
    ones = _query_ones()[:tq]
    order = jnp.argsort(stats[:, -1, STAT_BMAX, :N_HEADS], axis=-1).astype(jnp.int32)
    key_block = lambda j, r: jnp.maximum(j - r, 0)
    grid_spec = pltpu.PrefetchScalarGridSpec(
        num_scalar_prefetch=2, grid=(b, nq, nq),
        in_specs=[pl.BlockSpec((1, N_PAIRS, tq, LANES), lambda bi, j, r, *_: (bi, 0, j, 0)),
                  pl.BlockSpec((1, N_HEADS, tq, LANES), lambda bi, j, r, *_: (bi, 0, key_block(j, r), 0)),
                  pl.BlockSpec((1, N_HEADS, HEAD_DIM, tq), lambda bi, j, r, *_: (bi, 0, 0, key_block(j, r))),
                  pl.BlockSpec((tq, LANES), lambda bi, j, r, *_: (0, 0))],
        out_specs=pl.BlockSpec((1, tq, N_PAIRS * LANES), lambda bi, j, r, *_: (bi, j, 0)),
        scratch_shapes=[pltpu.VMEM((N_HEADS, tq, LANES), BF16),
                        pltpu.VMEM((N_HEADS, tq), F32), pltpu.VMEM((N_HEADS, tq), F32),
                        pltpu.VMEM((N_HEADS, HEAD_DIM, tq), F32)])
    return pl.pallas_call(
        _prompt_attn_kernel, grid_spec=grid_spec,
        out_shape=jax.ShapeDtypeStruct((b, t, N_PAIRS * LANES), BF16),
        compiler_params=pltpu.CompilerParams(
            dimension_semantics=("parallel", "parallel", "arbitrary"), vmem_limit_bytes=VMEM_LIMIT_BYTES),
        name="prompt_attention",
    )(stats[:, :, :N_STATS, :N_HEADS].reshape(-1), order.reshape(-1), q, ka, vt, ones)


def _sample_attn_kernel(q_ref, pk_ref, pv_ref, kn_ref, vn_ref, cpt_ref, cnt_ref, o_ref,
                        qw_ref, m_ref, l_ref, acc_ref, *, n_past):
    i = pl.program_id(1)
    tq = q_ref.shape[2]
    rows = 2 * tq

    @pl.when(i == 0)
    def _():
        m_ref[...] = jnp.full(m_ref.shape, -jnp.inf, F32)
        l_ref[...] = jnp.zeros(l_ref.shape, F32)
        acc_ref[...] = jnp.zeros(acc_ref.shape, F32)
        lane = lax.broadcasted_iota(jnp.int32, (tq, LANES), 1)
        for hp in range(N_PAIRS):
            pair = q_ref[0, hp]
            for e in range(2):
                qw_ref[hp, e * tq:(e + 1) * tq, :] = jnp.where(_own_lanes(lane, e), pair, jnp.zeros_like(pair))

    def step(k_of, v_of, ckt, masked):
        tk = ckt.shape[1]
        s = jnp.concatenate(
            [lax.dot_general(qw_ref[hp], k_of(hp), (((1,), (1,)), ((), ())), preferred_element_type=F32)
             for hp in range(N_PAIRS)], axis=0)
        ck = jnp.concatenate([jnp.broadcast_to(ckt[h:h + 1, :], (tq, tk)) for h in range(N_HEADS)], axis=0)
        s = s - ck * LOG2E
        if masked:
            q_idx = lax.rem(lax.broadcasted_iota(jnp.int32, s.shape, 0), tq)
            s = jnp.where(lax.broadcasted_iota(jnp.int32, s.shape, 1) <= q_idx, s, -jnp.inf)
        m_old = m_ref[...]
        m_new = jnp.maximum(m_old, jnp.max(s, axis=1, keepdims=True))
        alpha = jnp.exp2(m_old - m_new)
        p = jnp.exp2(s - m_new)
        l_ref[...] = alpha * l_ref[...] + jnp.sum(p, axis=1, keepdims=True)
        m_ref[...] = m_new
        pb = p.astype(BF16)
        for hp in range(N_PAIRS):
            rs = slice(hp * rows, (hp + 1) * rows)
            acc_ref[hp] = alpha[rs, :] * acc_ref[hp] + jnp.dot(pb[rs, :], v_of(hp), preferred_element_type=F32)

    pair_lanes = lambda hp: slice(hp * LANES, (hp + 1) * LANES)

    @pl.when(i < n_past)
    def _():
        step(lambda hp: pk_ref[0, :, pair_lanes(hp)].astype(BF16),
             lambda hp: pv_ref[0, :, pair_lanes(hp)].astype(BF16), cpt_ref[0], False)

    @pl.when(i == n_past)
    def _():
        step(lambda hp: kn_ref[0, hp], lambda hp: vn_ref[0, hp], cnt_ref[0], True)
        inv = 1.0 / l_ref[...]
        for hp in range(N_PAIRS):
            o = acc_ref[hp] * inv[hp * rows:(hp + 1) * rows, :]
            for e in range(2):
                lo = hp * LANES + e * HEAD_DIM
                o_ref[0, :, lo:lo + HEAD_DIM] = o[e * tq:(e + 1) * tq, e * HEAD_DIM:(e + 1) * HEAD_DIM].astype(BF16)


def _sample_attention(q, past_k, past_v, kn, vn, cpt, cnt):
    b, _, tq, _ = q.shape
    p_len, a_dim = past_k.shape[1], past_k.shape[2]
    tk = min(SAMPLE_KEY_BLOCK, p_len)
    n_past = p_len // tk
    past = lambda bi, i: (bi, jnp.minimum(i, n_past - 1), 0)
    whole4 = pl.BlockSpec((1, N_PAIRS, tq, LANES), lambda bi, i: (bi, 0, 0, 0))
    return pl.pallas_call(
        functools.partial(_sample_attn_kernel, n_past=n_past),
        grid=(b, n_past + 1),
        in_specs=[whole4,
                  pl.BlockSpec((1, tk, a_dim), past), pl.BlockSpec((1, tk, a_dim), past),
                  whole4, whole4,
                  pl.BlockSpec((1, N_HEADS, tk), lambda bi, i: (bi, 0, jnp.minimum(i, n_past - 1))),
                  pl.BlockSpec((1, N_HEADS, tq), lambda bi, i: (bi, 0, 0))],
        out_specs=pl.BlockSpec((1, tq, a_dim), lambda bi, i: (bi, 0, 0)),
        out_shape=jax.ShapeDtypeStruct((b, tq, a_dim), BF16),
        scratch_shapes=[pltpu.VMEM((N_PAIRS, 2 * tq, LANES), BF16),
                        pltpu.VMEM((N_HEADS * tq, 1), F32), pltpu.VMEM((N_HEADS * tq, 1), F32),
                        pltpu.VMEM((N_PAIRS, 2 * tq, LANES), F32)],
        compiler_params=pltpu.CompilerParams(
            dimension_semantics=("parallel", "arbitrary"), vmem_limit_bytes=VMEM_LIMIT_BYTES),
        name="sample_attention",
    )(q, past_k, past_v, kn, vn, cpt, cnt)


def _merge_ffn_kernel(x_ref, ylru_ref, yatt_ref, nmix_ref, wg_ref, bgate_ref, wbl_ref, wba_ref, wout_ref,
                      nffn_ref, wfi_ref, wfo_ref, nfin_ref, y_ref):
    d = x_ref.shape[1]
    d_ff = wfo_ref.shape[0]
    x = x_ref[...]
    xb = _rmsnorm(x, nmix_ref[...]).astype(BF16)
    g = jax.nn.sigmoid(jnp.dot(xb, wg_ref[...], preferred_element_type=F32) + bgate_ref[...])
    mixed = (g[:, :d] * jnp.dot(ylru_ref[...], wbl_ref[...], preferred_element_type=F32)
             + g[:, d:] * jnp.dot(yatt_ref[...], wba_ref[...], preferred_element_type=F32))
    x = x + jnp.dot(mixed.astype(BF16), wout_ref[...], preferred_element_type=F32)
    xb2 = _rmsnorm(x, nffn_ref[...]).astype(BF16)
    hid = jnp.dot(xb2, wfi_ref[...], preferred_element_type=F32)
    gf, up = hid[:, :d_ff], hid[:, d_ff:]
    act = (gf * jax.nn.sigmoid(gf) * up).astype(BF16)
    x = x + jnp.dot(act, wfo_ref[...], preferred_element_type=F32)
    y_ref[...] = _rmsnorm(x, nfin_ref[...])


def _merge_ffn(x, ylru, yatt, w, norm_final):
    n, d = x.shape
    tb = min(TOKEN_BLOCK, n)
    tok = pl.BlockSpec((tb, d), lambda i: (i, 0))
    consts = [w["norm_mix"], w["w_g"], w["b_gate"], w["w_br_lru"], w["w_br_att"], w["w_out"],
              w["norm_ffn"], w["w_ffn_in"], w["w_ffn_out"], norm_final]
    return pl.pallas_call(
        _merge_ffn_kernel,
        grid=(n // tb,),
        in_specs=[tok, tok, tok] + [_const_spec(c.shape) for c in consts],
        out_specs=tok,
        out_shape=jax.ShapeDtypeStruct((n, d), F32),
        compiler_params=pltpu.CompilerParams(
            dimension_semantics=("parallel",), vmem_limit_bytes=VMEM_LIMIT_BYTES),
        name="merge_ffn",
    )(x, ylru, yatt, *consts)


def _prep_layer_weights(norm_mix, w_in, b_forget, b_gate, conv_w, conv_b, w_rg, b_rg, w_ig, b_ig, lam,
                        w_br_lru, w_br_att, w_out, norm_ffn, w_ffn_in, w_ffn_out):
    d = w_in.shape[0]
    a_dim = N_HEADS * HEAD_DIM
    n1 = 2 * d + 3 * a_dim
    row = lambda v: v.reshape(1, -1).astype(F32)
    w_f = jnp.pad(w_in[:, n1 + 2 * d:], ((0, 0), (0, LANES - N_HEADS)))
    per_group = MXU_DIM // (d // N_LRU_BLOCKS)

    def block_diag(wb):
        g = wb.reshape(-1, per_group, wb.shape[1], wb.shape[2])
        eye = jnp.eye(per_group, dtype=wb.dtype)
        return jnp.einsum("gaij,ab->gaibj", g, eye).reshape(g.shape[0], MXU_DIM, MXU_DIM)

    return {
        "norm_mix": row(norm_mix),
        "w_in1": jnp.concatenate([w_in[:, :n1], w_f], axis=1).astype(BF16),
        "w_g": w_in[:, n1:n1 + 2 * d].astype(BF16),
        "b_forget": jnp.pad(row(b_forget), ((0, 0), (0, LANES - N_HEADS))),
        "b_gate": row(b_gate),
        "conv_w": conv_w.astype(F32), "conv_b": row(conv_b),
        "w_gates": jnp.concatenate([block_diag(w_rg), block_diag(w_ig)], axis=2).astype(BF16),
        "b_rg": row(b_rg), "b_ig": row(b_ig), "lam": row(lam),
        "w_br_lru": w_br_lru.astype(BF16), "w_br_att": w_br_att.astype(BF16), "w_out": w_out.astype(BF16),
        "norm_ffn": row(norm_ffn), "w_ffn_in": w_ffn_in.astype(BF16), "w_ffn_out": w_ffn_out.astype(BF16),
    }


def _layer_prompt(x, w, norm_final):
    b, t, d = x.shape
    zeros_conv = jnp.zeros((b, CONV_WIDTH - 1, d), F32)
    zeros_h = jnp.zeros((b, 1, d), F32)
    q, kb, _, vt, kf, vf, logf, ylru, conv_o, h_last = _proj_lru(x, zeros_conv, zeros_h, w, reset_first=True)
    ka, stats = _attn_prep(logf, kb, q)
    yatt = _prompt_attention(q, ka, vt, stats)
    y = _merge_ffn(x.reshape(b * t, d), ylru.reshape(b * t, d), yatt.reshape(b * t, d), w, norm_final)
    return y.reshape(b, t, d), kf, vf, logf[..., :N_HEADS], conv_o, h_last[:, 0]


def _layer_sample(x, conv0, h0, past_k, past_v, past_logf, w, norm_final):
    b, t, d = x.shape
    p_len = past_k.shape[1]
    a_dim = N_HEADS * HEAD_DIM
    q, kb, vb, _, kf, vf, logf, ylru, conv_o, h_last = _proj_lru(x, conv0, h0[:, None, :], w, reset_first=False)
    past_lf = jnp.pad(past_logf.astype(F32), ((0, 0), (0, 0), (0, LANES - N_HEADS)))
    c_past, ct_past = _cumsum_heads(past_lf, jnp.zeros((b, 1, LANES), F32))
    _, ct_new = _cumsum_heads(logf, c_past[:, p_len - 1:p_len, :])
    yatt = _sample_attention(q, past_k.reshape(b, p_len, a_dim), past_v.reshape(b, p_len, a_dim), kb, vb,
                             ct_past, ct_new)
    y = _merge_ffn(x.reshape(b * t, d), ylru.reshape(b * t, d), yatt.reshape(b * t, d), w, norm_final)
    return y.reshape(b, t, d), kf, vf, logf[..., :N_HEADS], conv_o, h_last[:, 0]


def kernel(x_prompt, x_sample, cache_k, cache_v, cache_logf, state_conv, state_h, norm_mix, w_in, b_forget,
           b_gate, conv_w, conv_b, w_rg, b_rg, w_ig, b_ig, lru_lambda, w_br_lru, w_br_att, w_out, norm_ffn,
           w_ffn_in, w_ffn_out, norm_final):
    assert norm_mix.shape[0] == 1, "single-layer trunk: the final rmsnorm is fused into the layer's last stage"
    heads = lambda kv: kv.reshape(kv.shape[0], kv.shape[1], N_HEADS, HEAD_DIM)[None]
    nfin = norm_final.reshape(1, -1).astype(F32)
    w = _prep_layer_weights(norm_mix[0], w_in[0], b_forget[0], b_gate[0], conv_w[0], conv_b[0], w_rg[0], b_rg[0],
                            w_ig[0], b_ig[0], lru_lambda[0], w_br_lru[0], w_br_att[0], w_out[0], norm_ffn[0],
                            w_ffn_in[0], w_ffn_out[0])
    yp, kp, vp, lfp, cp, hp = _layer_prompt(x_prompt, w, nfin)
    ys, ks, vs, lfs, cs, hs = _layer_sample(x_sample, state_conv[0], state_h[0], cache_k[0], cache_v[0],
                                            cache_logf[0], w, nfin)
    return (yp, ys, heads(kp), heads(vp), lfp[None], cp[None], hp[None],
            heads(ks), heads(vs), lfs[None], cs[None], hs[None])
```

```python
import functools
import math

import numpy as np
import jax
import jax.numpy as jnp
from jax import lax
from jax.experimental import pallas as pl
from jax.experimental.pallas import tpu as pltpu

N_HEADS = 16
HEAD_DIM = 64
N_LRU_BLOCKS = 16
CONV_WIDTH = 4
LRU_C = 8.0
EPS = 1e-6

LANES = 128
MXU_DIM = 256
VMEM_LIMIT_BYTES = 56 * 1024 * 1024
CARRY_ROWS = 8

N_PAIRS = N_HEADS // 2
LOG2E = math.log2(math.e)
N_SPLIT = 3
ATTN_BLOCK = 512
TOKEN_BLOCK = 256
SAMPLE_KEY_BLOCK = 1024
CUMSUM_BLOCK = 2048

STAT_BASE, STAT_BMAX, STAT_KNORM, STAT_QNORM, N_STATS = 0, 1, 2, 3, 4
STAT_ROWS = 8
NORM_SLACK = 1.01
EXP2_ZERO_BELOW = -150.0
SKIP_MARGIN = 4.0

F32 = jnp.float32
BF16 = jnp.bfloat16
U32 = jnp.uint32


def _log_sigmoid(x):
    return jnp.minimum(x, 0.0) - jnp.log1p(jnp.exp(-jnp.abs(x)))


def _gelu_tanh(x):
    c = math.sqrt(2.0 / math.pi)
    return 0.5 * x * (1.0 + jnp.tanh(c * (x + 0.044715 * (x * x * x))))


def _rmsnorm(x, g):
    return x * lax.rsqrt(jnp.mean(x * x, axis=-1, keepdims=True) + EPS) * g


def _scan_affine_rows(a, b):
    n = a.shape[0]
    row = lax.broadcasted_iota(jnp.int32, a.shape, 0)
    d = 1
    while d < n:
        keep = row >= d
        a_sh = jnp.where(keep, pltpu.roll(a, d, 0), 1.0)
        b_sh = jnp.where(keep, pltpu.roll(b, d, 0), 0.0)
        b = a * b_sh + b
        a = a * a_sh
        d *= 2
    return a, b


def _cumsum_rows(x):
    n = x.shape[0]
    row = lax.broadcasted_iota(jnp.int32, x.shape, 0)
    d = 1
    while d < n:
        x = x + jnp.where(row >= d, pltpu.roll(x, d, 0), 0.0)
        d *= 2
    return x


def _const_spec(shape):
    return pl.BlockSpec(shape, lambda *_: (0,) * len(shape), pipeline_mode=pl.Buffered(1))


def _own_lanes(lane, head):
    return (lane < HEAD_DIM) if head % 2 == 0 else (lane >= HEAD_DIM)


def _aug_lane0(head):
    return HEAD_DIM if head % 2 == 0 else 0


def _zero_from(x):
    u = pltpu.bitcast(x, U32)
    return pltpu.bitcast(lax.shift_right_logical(lax.shift_right_logical(u, U32(16)), U32(16)), F32)


def _proj_lru_kernel(x_ref, conv0_ref, h0_ref, nmix_ref, win_ref, convw_ref, convb_ref, wgate_ref,
                     brg_ref, big_ref, lam_ref, bf_ref,
                     q_ref, kb_ref, vb_ref, vt_ref, kf_ref, vf_ref, logf_ref, ylru_ref, convo_ref, hlast_ref,
                     xr_buf, h_carry, *, reset_first):
    t = pl.program_id(1)
    tb, d = x_ref.shape[1], x_ref.shape[2]
    a_dim = kf_ref.shape[2]

    xb = _rmsnorm(x_ref[0], nmix_ref[...]).astype(BF16)

    def proj(lo, width):
        return jnp.dot(xb, win_ref[:, lo:lo + width], preferred_element_type=F32)

    xr = proj(0, d)
    gate = proj(d, d)
    q = proj(2 * d, a_dim)
    k = proj(2 * d + a_dim, a_dim)
    v = proj(2 * d + 2 * a_dim, a_dim)
    f_logit = proj(2 * d + 3 * a_dim, LANES)

    qb = (q * (LOG2E * HEAD_DIM ** -0.5)).astype(BF16)
    kb, vb, vtb = k.astype(BF16), v.astype(BF16), v.T.astype(BF16)
    for hp in range(N_PAIRS):
        ls = slice(hp * LANES, (hp + 1) * LANES)
        q_ref[0, hp] = qb[:, ls]
        kb_ref[0, hp] = kb[:, ls]
        vb_ref[0, hp] = vb[:, ls]
    for h in range(N_HEADS):
        vt_ref[0, h] = vtb[h * HEAD_DIM:(h + 1) * HEAD_DIM, :]
    kf_ref[0] = k
    vf_ref[0] = v
    logf_ref[0] = _log_sigmoid(f_logit + bf_ref[...])

    lo = CARRY_ROWS - (CONV_WIDTH - 1)

    @pl.when(t == 0)
    def _():
        xr_buf[lo:CARRY_ROWS, :] = conv0_ref[0]
        h_carry[...] = h0_ref[0]

    xr_buf[CARRY_ROWS:CARRY_ROWS + tb, :] = xr
    xc = convb_ref[...] + xr_buf[lo:lo + tb, :] * convw_ref[0:1, :]
    for j in range(1, CONV_WIDTH):
        xc = xc + xr_buf[lo + j:lo + j + tb, :] * convw_ref[j:j + 1, :]
    tail = xr_buf[tb + lo:tb + CARRY_ROWS, :]
    convo_ref[0] = tail
    xr_buf[lo:CARRY_ROWS, :] = tail

    xcb = xc.astype(BF16)
    parts = [jnp.dot(xcb[:, g * MXU_DIM:(g + 1) * MXU_DIM], wgate_ref[g], preferred_element_type=F32)
             for g in range(d // MXU_DIM)]
    r = jax.nn.sigmoid(jnp.concatenate([p[:, :MXU_DIM] for p in parts], axis=1) + brg_ref[...])
    i = jax.nn.sigmoid(jnp.concatenate([p[:, MXU_DIM:] for p in parts], axis=1) + big_ref[...])
    a = jnp.exp(LRU_C * r * _log_sigmoid(lam_ref[...]))
    mult = jnp.sqrt(1.0 - a * a)
    if reset_first:
        row = lax.broadcasted_iota(jnp.int32, mult.shape, 0)
        mult = jnp.where(jnp.logical_and(row == 0, t == 0), 1.0, mult)
    a_cum, b_cum = _scan_affine_rows(a, mult * i * xc)
    h = a_cum * h_carry[...] + b_cum
    h_carry[...] = h[tb - 1:tb, :]
    hlast_ref[0] = h[tb - 1:tb, :]
    ylru_ref[0] = (h * _gelu_tanh(gate)).astype(BF16)


def _proj_lru(x, conv0, h0, w, *, reset_first):
    b, t, d = x.shape
    a_dim = N_HEADS * HEAD_DIM
    tb = min(TOKEN_BLOCK, t)
    grid = (b, t // tb)
    tok = lambda width: pl.BlockSpec((1, tb, width), lambda bi, ti: (bi, ti, 0))
    per_b = lambda rows: pl.BlockSpec((1, rows, d), lambda bi, ti: (bi, 0, 0))
    pair_major = pl.BlockSpec((1, N_PAIRS, tb, LANES), lambda bi, ti: (bi, 0, ti, 0))
    out_shapes = (
        jax.ShapeDtypeStruct((b, N_PAIRS, t, LANES), BF16),
        jax.ShapeDtypeStruct((b, N_PAIRS, t, LANES), BF16),
        jax.ShapeDtypeStruct((b, N_PAIRS, t, LANES), BF16),
        jax.ShapeDtypeStruct((b, N_HEADS, HEAD_DIM, t), BF16),
        jax.ShapeDtypeStruct((b, t, a_dim), F32),
        jax.ShapeDtypeStruct((b, t, a_dim), F32),
        jax.ShapeDtypeStruct((b, t, LANES), F32),
        jax.ShapeDtypeStruct((b, t, d), BF16),
        jax.ShapeDtypeStruct((b, CONV_WIDTH - 1, d), F32),
        jax.ShapeDtypeStruct((b, 1, d), F32),
    )
    out_specs = (
        pair_major, pair_major, pair_major,
        pl.BlockSpec((1, N_HEADS, HEAD_DIM, tb), lambda bi, ti: (bi, 0, 0, ti)),
        tok(a_dim), tok(a_dim), tok(LANES), tok(d),
        per_b(CONV_WIDTH - 1), per_b(1),
    )
    in_specs = [
        tok(d), per_b(CONV_WIDTH - 1), per_b(1),
        _const_spec(w["norm_mix"].shape), _const_spec(w["w_in1"].shape),
        _const_spec(w["conv_w"].shape), _const_spec(w["conv_b"].shape), _const_spec(w["w_gates"].shape),
        _const_spec(w["b_rg"].shape), _const_spec(w["b_ig"].shape), _const_spec(w["lam"].shape),
        _const_spec(w["b_forget"].shape),
    ]
    return pl.pallas_call(
        functools.partial(_proj_lru_kernel, reset_first=reset_first),
        grid=grid, in_specs=in_specs, out_specs=out_specs, out_shape=out_shapes,
        scratch_shapes=[pltpu.VMEM((CARRY_ROWS + tb, d), F32), pltpu.VMEM((1, d), F32)],
        compiler_params=pltpu.CompilerParams(
            dimension_semantics=("arbitrary", "arbitrary"), vmem_limit_bytes=VMEM_LIMIT_BYTES),
        name="proj_lru",
    )(x, conv0, h0, w["norm_mix"], w["w_in1"], w["conv_w"], w["conv_b"], w["w_gates"],
      w["b_rg"], w["b_ig"], w["lam"], w["b_forget"])


def _bias_placement():
    place = np.zeros((N_SPLIT * LANES, N_PAIRS * LANES), np.float32)
    for h in range(N_HEADS):
        for s in range(N_SPLIT):
            place[s * LANES + h, (h // 2) * LANES + _aug_lane0(h) + s] = 1.0
    return jnp.asarray(place, BF16)


def _head_lane_sum():
    sel = np.zeros((N_PAIRS * LANES, LANES), np.float32)
    for h in range(N_HEADS):
        lo = (h // 2) * LANES + (h % 2) * HEAD_DIM
        sel[lo:lo + HEAD_DIM, h] = 1.0
    return jnp.asarray(sel, BF16)


def _query_ones():
    ones = np.zeros((ATTN_BLOCK, LANES), np.float32)
    for h in range(2):
        ones[:, _aug_lane0(h):_aug_lane0(h) + N_SPLIT] = 1.0
    return jnp.asarray(ones, BF16)


def _max_head_norm(slabs_ref, sel_ref):
    sq = jnp.concatenate([jnp.square(slabs_ref[0, hp].astype(F32)) for hp in range(N_PAIRS)], axis=1)
    norm2 = jnp.dot(sq.astype(BF16), sel_ref[...], preferred_element_type=F32)
    return jnp.sqrt(jnp.max(norm2, axis=0, keepdims=True) * NORM_SLACK)


def _attn_prep_kernel(lf_ref, k_ref, q_ref, place_ref, sel_ref, ka_ref, st_ref, carry):
    @pl.when(pl.program_id(1) == 0)
    def _():
        carry[...] = jnp.zeros(carry.shape, F32)

    n = lf_ref.shape[1]
    c = _cumsum_rows(lf_ref[0]) + carry[...]
    carry[...] = c[n - 1:n, :]
    base = c[0:1, :]
    rest = (base - c) * LOG2E
    pieces = []
    for _ in range(N_SPLIT):
        piece = rest.astype(BF16)
        pieces.append(piece)
        rest = rest - piece.astype(F32)
    placed = jnp.dot(jnp.concatenate(pieces, axis=1), place_ref[...], preferred_element_type=F32)
    lane = lax.broadcasted_iota(jnp.int32, (n, LANES), 1)
    for h in range(N_HEADS):
        bias = placed[:, (h // 2) * LANES:(h // 2 + 1) * LANES].astype(BF16)
        ka_ref[0, h] = jnp.where(_own_lanes(lane, h), k_ref[0, h // 2], bias)

    stats = {STAT_BASE: base * LOG2E,
             STAT_BMAX: c[n - 1:n, :] * -LOG2E,
             STAT_KNORM: _max_head_norm(k_ref, sel_ref),
             STAT_QNORM: _max_head_norm(q_ref, sel_ref)}
    rows = [stats.get(r, jnp.zeros((1, LANES), F32)) for r in range(STAT_ROWS)]
    st_ref[0, 0] = jnp.concatenate(rows, axis=0)


def _attn_prep(logf, kb, q):
    b, t, _ = logf.shape
    tc = min(ATTN_BLOCK, t)
    place, sel = _bias_placement(), _head_lane_sum()
    pair_major = pl.BlockSpec((1, N_PAIRS, tc, LANES), lambda bi, ti: (bi, 0, ti, 0))
    return pl.pallas_call(
        _attn_prep_kernel,
        grid=(b, t // tc),
        in_specs=[pl.BlockSpec((1, tc, LANES), lambda bi, ti: (bi, ti, 0)), pair_major, pair_major,
                  _const_spec(place.shape), _const_spec(sel.shape)],
        out_specs=(pl.BlockSpec((1, N_HEADS, tc, LANES), lambda bi, ti: (bi, 0, ti, 0)),
                   pl.BlockSpec((1, 1, STAT_ROWS, LANES), lambda bi, ti: (bi, ti, 0, 0))),
        out_shape=(jax.ShapeDtypeStruct((b, N_HEADS, t, LANES), BF16),
                   jax.ShapeDtypeStruct((b, t // tc, STAT_ROWS, LANES), F32)),
        scratch_shapes=[pltpu.VMEM((1, LANES), F32)],
        compiler_params=pltpu.CompilerParams(dimension_semantics=("arbitrary", "arbitrary")),
        name="attn_prep",
    )(logf, kb, q, place, sel)


def _cumsum_kernel(lf_ref, off_ref, c_ref, ct_ref, carry):
    @pl.when(pl.program_id(1) == 0)
    def _():
        carry[...] = off_ref[0]

    n = lf_ref.shape[1]
    c = _cumsum_rows(lf_ref[0]) + carry[...]
    carry[...] = c[n - 1:n, :]
    c_ref[0] = c
    ct_ref[0] = c.T


def _cumsum_heads(logf, offset):
    b, t, _ = logf.shape
    tc = min(CUMSUM_BLOCK, t)
    return pl.pallas_call(
        _cumsum_kernel,
        grid=(b, t // tc),
        in_specs=[pl.BlockSpec((1, tc, LANES), lambda bi, ti: (bi, ti, 0)),
                  pl.BlockSpec((1, 1, LANES), lambda bi, ti: (bi, 0, 0))],
        out_specs=(pl.BlockSpec((1, tc, LANES), lambda bi, ti: (bi, ti, 0)),
                   pl.BlockSpec((1, LANES, tc), lambda bi, ti: (bi, 0, ti))),
        out_shape=(jax.ShapeDtypeStruct((b, t, LANES), F32), jax.ShapeDtypeStruct((b, LANES, t), F32)),
        scratch_shapes=[pltpu.VMEM((1, LANES), F32)],
        compiler_params=pltpu.CompilerParams(dimension_semantics=("arbitrary", "arbitrary")),
        name="cumsum_heads",
    )(logf, offset)


def _prompt_attn_kernel(st_ref, order_ref, q_ref, ka_ref, vt_ref, ones_ref, o_ref, qa_ref, m_ref, l_ref, acc_ref):
    bi, j, r = pl.program_id(0), pl.program_id(1), pl.program_id(2)
    nblk = pl.num_programs(1)
    kb = jnp.maximum(j - r, 0)
    tq, tk = q_ref.shape[2], ka_ref.shape[2]
    stat = lambda blk, which, h: st_ref[((bi * nblk + blk) * N_STATS + which) * N_HEADS + h]

    @pl.when(r == 0)
    def _():
        m_ref[...] = jnp.full(m_ref.shape, -jnp.inf, F32)
        l_ref[...] = jnp.zeros(l_ref.shape, F32)
        acc_ref[...] = jnp.zeros(acc_ref.shape, F32)
        lane = lax.broadcasted_iota(jnp.int32, (tq, LANES), 1)
        for h in range(N_HEADS):
            qa_ref[h] = jnp.where(_own_lanes(lane, h), q_ref[0, h // 2], ones_ref[...])

    def scores(h):
        return lax.dot_general(ka_ref[0, h], qa_ref[h], (((1,), (1,)), ((), ())), preferred_element_type=F32)

    def heads_step(heads, masked):
        if masked:
            allowed = (lax.broadcasted_iota(jnp.int32, (tk, tq), 0) <= lax.broadcasted_iota(jnp.int32, (tk, tq), 1))
        s_next = scores(heads[0])
        for n, h in enumerate(heads):
            s = s_next
            if n + 1 < len(heads):
                s_next = scores(heads[n + 1])
            if masked:
                s = jnp.where(allowed, s, -jnp.inf)
            base = stat(kb, STAT_BASE, h)
            m_old = m_ref[pl.ds(h, 1), :]
            m_new = jnp.maximum(m_old, jnp.max(s, axis=0, keepdims=True) - base)
            alpha = jnp.exp2(m_old - m_new)
            p = jnp.exp2(s - (m_new + base))
            l_ref[pl.ds(h, 1), :] = alpha * l_ref[pl.ds(h, 1), :] + jnp.sum(p, axis=0, keepdims=True)
            if n + 1 < len(heads):
                alpha = alpha + _zero_from(s_next[tk - 8:tk, tq - LANES:tq])[0:1, 0:1]
            acc_ref[h] = alpha * acc_ref[h] + jnp.dot(vt_ref[0, h], p.astype(BF16), preferred_element_type=F32)
            m_ref[pl.ds(h, 1), :] = m_new

    def is_live(h):
        bmax = stat(kb, STAT_BMAX, h)
        bound = (stat(j, STAT_QNORM, h) * stat(kb, STAT_KNORM, h) + bmax
                 + jnp.abs(bmax) * 2.0 ** -20 + SKIP_MARGIN)
        return bound - jnp.min(m_ref[pl.ds(h, 1), :]) > EXP2_ZERO_BELOW

    @pl.when(r == 0)
    def _():
        heads_step(tuple(range(N_HEADS)), True)

    @pl.when(jnp.logical_and(r > 0, r <= j))
    def _():
        heads = [order_ref[bi * N_HEADS + n] for n in range(N_HEADS)]
        live = [is_live(h) for h in heads]
        pair_live = [jnp.logical_or(live[2 * slot], live[2 * slot + 1]) for slot in range(N_PAIRS)]
        all_live = functools.reduce(jnp.logical_and, pair_live)

        @pl.when(all_live)
        def _():
            heads_step(tuple(range(N_HEADS)), False)

        for slot in range(N_PAIRS):
            @pl.when(jnp.logical_and(pair_live[slot], jnp.logical_not(all_live)))
            def _():
                heads_step((heads[2 * slot], heads[2 * slot + 1]), False)

    @pl.when(r == j)
    def _():
        inv = 1.0 / l_ref[...]
        for hp in range(N_PAIRS):
            o = jnp.concatenate([acc_ref[2 * hp + e] * inv[2 * hp + e:2 * hp + e + 1, :] for e in range(2)], axis=0)
            o_ref[0, :, hp * LANES:(hp + 1) * LANES] = o.T.astype(BF16)


def _prompt_attention(q, ka, vt, stats):
    b, _, t, _ = q.shape
    tq = min(ATTN_BLOCK, t)
    nq = t // tq
    ones = _query_ones()[:tq]
    order = jnp.argsort(stats[:, -1, STAT_BMAX, :N_HEADS], axis=-1).astype(jnp.int32)
    key_block = lambda j, r: jnp.maximum(j - r, 0)
    grid_spec = pltpu.PrefetchScalarGridSpec(
        num_scalar_prefetch=2, grid=(b, nq, nq),
        in_specs=[pl.BlockSpec((1, N_PAIRS, tq, LANES), lambda bi, j, r, *_: (bi, 0, j, 0)),
                  pl.BlockSpec((1, N_HEADS, tq, LANES), lambda bi, j, r, *_: (bi, 0, key_block(j, r), 0)),
                  pl.BlockSpec((1, N_HEADS, HEAD_DIM, tq), lambda bi, j, r, *_: (bi, 0, 0, key_block(j, r))),
                  pl.BlockSpec((tq, LANES), lambda bi, j, r, *_: (0, 0))],
        out_specs=pl.BlockSpec((1, tq, N_PAIRS * LANES), lambda bi, j, r, *_: (bi, j, 0)),
        scratch_shapes=[pltpu.VMEM((N_HEADS, tq, LANES), BF16),
                        pltpu.VMEM((N_HEADS, tq), F32), pltpu.VMEM((N_HEADS, tq), F32),
                        pltpu.VMEM((N_HEADS, HEAD_DIM, tq), F32)])
    return pl.pallas_call(
        _prompt_attn_kernel, grid_spec=grid_spec,
        out_shape=jax.ShapeDtypeStruct((b, t, N_PAIRS * LANES), BF16),
        compiler_params=pltpu.CompilerParams(
            dimension_semantics=("parallel", "parallel", "arbitrary"), vmem_limit_bytes=VMEM_LIMIT_BYTES),
        name="prompt_attention",
    )(stats[:, :, :N_STATS, :N_HEADS].reshape(-1), order.reshape(-1), q, ka, vt, ones)


def _sample_attn_kernel(q_ref, pk_ref, pv_ref, kn_ref, vn_ref, cpt_ref, cnt_ref, o_ref,
                        qw_ref, m_ref, l_ref, acc_ref, *, n_past):
    i = pl.program_id(1)
    tq = q_ref.shape[2]
    rows = 2 * tq

    @pl.when(i == 0)
    def _():
        m_ref[...] = jnp.full(m_ref.shape, -jnp.inf, F32)
        l_ref[...] = jnp.zeros(l_ref.shape, F32)
        acc_ref[...] = jnp.zeros(acc_ref.shape, F32)
        lane = lax.broadcasted_iota(jnp.int32, (tq, LANES), 1)
        for hp in range(N_PAIRS):
            pair = q_ref[0, hp]
            for e in range(2):
                qw_ref[hp, e * tq:(e + 1) * tq, :] = jnp.where(_own_lanes(lane, e), pair, jnp.zeros_like(pair))

    def step(k_of, v_of, ckt, masked):
        tk = ckt.shape[1]
        s = jnp.concatenate(
            [lax.dot_general(qw_ref[hp], k_of(hp), (((1,), (1,)), ((), ())), preferred_element_type=F32)
             for hp in range(N_PAIRS)], axis=0)
        ck = jnp.concatenate([jnp.broadcast_to(ckt[h:h + 1, :], (tq, tk)) for h in range(N_HEADS)], axis=0)
        s = s - ck * LOG2E
        if masked:
            q_idx = lax.rem(lax.broadcasted_iota(jnp.int32, s.shape, 0), tq)
            s = jnp.where(lax.broadcasted_iota(jnp.int32, s.shape, 1) <= q_idx, s, -jnp.inf)
        m_old = m_ref[...]
        m_new = jnp.maximum(m_old, jnp.max(s, axis=1, keepdims=True))
        alpha = jnp.exp2(m_old - m_new)
        p = jnp.exp2(s - m_new)
        l_ref[...] = alpha * l_ref[...] + jnp.sum(p, axis=1, keepdims=True)
        m_ref[...] = m_new
        pb = p.astype(BF16)
        for hp in range(N_PAIRS):
            rs = slice(hp * rows, (hp + 1) * rows)
            acc_ref[hp] = alpha[rs, :] * acc_ref[hp] + jnp.dot(pb[rs, :], v_of(hp), preferred_element_type=F32)

    pair_lanes = lambda hp: slice(hp * LANES, (hp + 1) * LANES)

    @pl.when(i < n_past)
    def _():
        step(lambda hp: pk_ref[0, :, pair_lanes(hp)].astype(BF16),
             lambda hp: pv_ref[0, :, pair_lanes(hp)].astype(BF16), cpt_ref[0], False)

    @pl.when(i == n_past)
    def _():
        step(lambda hp: kn_ref[0, hp], lambda hp: vn_ref[0, hp], cnt_ref[0], True)
        inv = 1.0 / l_ref[...]
        for hp in range(N_PAIRS):
            o = acc_ref[hp] * inv[hp * rows:(hp + 1) * rows, :]
            for e in range(2):
                lo = hp * LANES + e * HEAD_DIM
                o_ref[0, :, lo:lo + HEAD_DIM] = o[e * tq:(e + 1) * tq, e * HEAD_DIM:(e + 1) * HEAD_DIM].astype(BF16)


def _sample_attention(q, past_k, past_v, kn, vn, cpt, cnt):
    b, _, tq, _ = q.shape
    p_len, a_dim = past_k.shape[1], past_k.shape[2]
    tk = min(SAMPLE_KEY_BLOCK, p_len)
    n_past = p_len // tk
    past = lambda bi, i: (bi, jnp.minimum(i, n_past - 1), 0)
    whole4 = pl.BlockSpec((1, N_PAIRS, tq, LANES), lambda bi, i: (bi, 0, 0, 0))
    return pl.pallas_call(
        functools.partial(_sample_attn_kernel, n_past=n_past),
        grid=(b, n_past + 1),
        in_specs=[whole4,
                  pl.BlockSpec((1, tk, a_dim), past), pl.BlockSpec((1, tk, a_dim), past),
                  whole4, whole4,
                  pl.BlockSpec((1, N_HEADS, tk), lambda bi, i: (bi, 0, jnp.minimum(i, n_past - 1))),
                  pl.BlockSpec((1, N_HEADS, tq), lambda bi, i: (bi, 0, 0))],
        out_specs=pl.BlockSpec((1, tq, a_dim), lambda bi, i: (bi, 0, 0)),
        out_shape=jax.ShapeDtypeStruct((b, tq, a_dim), BF16),
        scratch_shapes=[pltpu.VMEM((N_PAIRS, 2 * tq, LANES), BF16),
                        pltpu.VMEM((N_HEADS * tq, 1), F32), pltpu.VMEM((N_HEADS * tq, 1), F32),
                        pltpu.VMEM((N_PAIRS, 2 * tq, LANES), F32)],
        compiler_params=pltpu.CompilerParams(
            dimension_semantics=("parallel", "arbitrary"), vmem_limit_bytes=VMEM_LIMIT_BYTES),
        name="sample_attention",
    )(q, past_k, past_v, kn, vn, cpt, cnt)


def _merge_ffn_kernel(x_ref, ylru_ref, yatt_ref, nmix_ref, wg_ref, bgate_ref, wbl_ref, wba_ref, wout_ref,
                      nffn_ref, wfi_ref, wfo_ref, nfin_ref, y_ref):
    d = x_ref.shape[1]
    d_ff = wfo_ref.shape[0]
    x = x_ref[...]
    xb = _rmsnorm(x, nmix_ref[...]).astype(BF16)
    g = jax.nn.sigmoid(jnp.dot(xb, wg_ref[...], preferred_element_type=F32) + bgate_ref[...])
    mixed = (g[:, :d] * jnp.dot(ylru_ref[...], wbl_ref[...], preferred_element_type=F32)
             + g[:, d:] * jnp.dot(yatt_ref[...], wba_ref[...], preferred_element_type=F32))
    x = x + jnp.dot(mixed.astype(BF16), wout_ref[...], preferred_element_type=F32)
    xb2 = _rmsnorm(x, nffn_ref[...]).astype(BF16)
    hid = jnp.dot(xb2, wfi_ref[...], preferred_element_type=F32)
    gf, up = hid[:, :d_ff], hid[:, d_ff:]
    act = (gf * jax.nn.sigmoid(gf) * up).astype(BF16)
    x = x + jnp.dot(act, wfo_ref[...], preferred_element_type=F32)
    y_ref[...] = _rmsnorm(x, nfin_ref[...])


def _merge_ffn(x, ylru, yatt, w, norm_final):
    n, d = x.shape
    tb = min(TOKEN_BLOCK, n)
    tok = pl.BlockSpec((tb, d), lambda i: (i, 0))
    consts = [w["norm_mix"], w["w_g"], w["b_gate"], w["w_br_lru"], w["w_br_att"], w["w_out"],
              w["norm_ffn"], w["w_ffn_in"], w["w_ffn_out"], norm_final]
    return pl.pallas_call(
        _merge_ffn_kernel,
        grid=(n // tb,),
        in_specs=[tok, tok, tok] + [_const_spec(c.shape) for c in consts],
        out_specs=tok,
        out_shape=jax.ShapeDtypeStruct((n, d), F32),
        compiler_params=pltpu.CompilerParams(
            dimension_semantics=("parallel",), vmem_limit_bytes=VMEM_LIMIT_BYTES),
        name="merge_ffn",
    )(x, ylru, yatt, *consts)


def _prep_layer_weights(norm_mix, w_in, b_forget, b_gate, conv_w, conv_b, w_rg, b_rg, w_ig, b_ig, lam,
                        w_br_lru, w_br_att, w_out, norm_ffn, w_ffn_in, w_ffn_out):
    d = w_in.shape[0]
    a_dim = N_HEADS * HEAD_DIM
    n1 = 2 * d + 3 * a_dim
    row = lambda v: v.reshape(1, -1).astype(F32)
    w_f = jnp.pad(w_in[:, n1 + 2 * d:], ((0, 0), (0, LANES - N_HEADS)))
    per_group = MXU_DIM // (d // N_LRU_BLOCKS)

    def block_diag(wb):
        g = wb.reshape(-1, per_group, wb.shape[1], wb.shape[2])
        eye = jnp.eye(per_group, dtype=wb.dtype)
        return jnp.einsum("gaij,ab->gaibj", g, eye).reshape(g.shape[0], MXU_DIM, MXU_DIM)

    return {
        "norm_mix": row(norm_mix),
        "w_in1": jnp.concatenate([w_in[:, :n1], w_f], axis=1).astype(BF16),
        "w_g": w_in[:, n1:n1 + 2 * d].astype(BF16),
        "b_forget": jnp.pad(row(b_forget), ((0, 0), (0, LANES - N_HEADS))),
        "b_gate": row(b_gate),
        "conv_w": conv_w.astype(F32), "conv_b": row(conv_b),
        "w_gates": jnp.concatenate([block_diag(w_rg), block_diag(w_ig)], axis=2).astype(BF16),
        "b_rg": row(b_rg), "b_ig": row(b_ig), "lam": row(lam),
        "w_br_lru": w_br_lru.astype(BF16), "w_br_att": w_br_att.astype(BF16), "w_out": w_out.astype(BF16),
        "norm_ffn": row(norm_ffn), "w_ffn_in": w_ffn_in.astype(BF16), "w_ffn_out": w_ffn_out.astype(BF16),
    }


def _layer_prompt(x, w, norm_final):
    b, t, d = x.shape
    zeros_conv = jnp.zeros((b, CONV_WIDTH - 1, d), F32)
    zeros_h = jnp.zeros((b, 1, d), F32)
    q, kb, _, vt, kf, vf, logf, ylru, conv_o, h_last = _proj_lru(x, zeros_conv, zeros_h, w, reset_first=True)
    ka, stats = _attn_prep(logf, kb, q)
    yatt = _prompt_attention(q, ka, vt, stats)
    y = _merge_ffn(x.reshape(b * t, d), ylru.reshape(b * t, d), yatt.reshape(b * t, d), w, norm_final)
    return y.reshape(b, t, d), kf, vf, logf[..., :N_HEADS], conv_o, h_last[:, 0]


def _layer_sample(x, conv0, h0, past_k, past_v, past_logf, w, norm_final):
    b, t, d = x.shape
    p_len = past_k.shape[1]
    a_dim = N_HEADS * HEAD_DIM
    q, kb, vb, _, kf, vf, logf, ylru, conv_o, h_last = _proj_lru(x, conv0, h0[:, None, :], w, reset_first=False)
    past_lf = jnp.pad(past_logf.astype(F32), ((0, 0), (0, 0), (0, LANES - N_HEADS)))
    c_past, ct_past = _cumsum_heads(past_lf, jnp.zeros((b, 1, LANES), F32))
    _, ct_new = _cumsum_heads(logf, c_past[:, p_len - 1:p_len, :])
    yatt = _sample_attention(q, past_k.reshape(b, p_len, a_dim), past_v.reshape(b, p_len, a_dim), kb, vb,
                             ct_past, ct_new)
    y = _merge_ffn(x.reshape(b * t, d), ylru.reshape(b * t, d), yatt.reshape(b * t, d), w, norm_final)
    return y.reshape(b, t, d), kf, vf, logf[..., :N_HEADS], conv_o, h_last[:, 0]


def kernel(x_prompt, x_sample, cache_k, cache_v, cache_logf, state_conv, state_h, norm_mix, w_in, b_forget,
           b_gate, conv_w, conv_b, w_rg, b_rg, w_ig, b_ig, lru_lambda, w_br_lru, w_br_att, w_out, norm_ffn,
           w_ffn_in, w_ffn_out, norm_final):
    assert norm_mix.shape[0] == 1, "single-layer trunk: the final rmsnorm is fused into the layer's last stage"
    heads = lambda kv: kv.reshape(kv.shape[0], kv.shape[1], N_HEADS, HEAD_DIM)[None]
    nfin = norm_final.reshape(1, -1).astype(F32)
    w = _prep_layer_weights(norm_mix[0], w_in[0], b_forget[0], b_gate[0], conv_w[0], conv_b[0], w_rg[0], b_rg[0],
                            w_ig[0], b_ig[0], lru_lambda[0], w_br_lru[0], w_br_att[0], w_out[0], norm_ffn[0],
                            w_ffn_in[0], w_ffn_out[0])
    yp, kp, vp, lfp, cp, hp = _layer_prompt(x_prompt, w, nfin)
    ys, ks, vs, lfs, cs, hs = _layer_sample(x_sample, state_conv[0], state_h[0], cache_k[0], cache_v[0],
                                            cache_logf[0], w, nfin)
    return (yp, ys, heads(kp), heads(vp), lfp[None], cp[None], hp[None],
            heads(ks), heads(vs), lfs[None], cs[None], hs[None])
```

```python
import functools
import math

import numpy as np
import jax
import jax.numpy as jnp
from jax import lax
from jax.experimental import pallas as pl
from jax.experimental.pallas import tpu as pltpu

N_HEADS = 16
HEAD_DIM = 64
N_LRU_BLOCKS = 16
CONV_WIDTH = 4
LRU_C = 8.0
EPS = 1e-6

LANES = 128
MXU_DIM = 256
VMEM_LIMIT_BYTES = 56 * 1024 * 1024
CARRY_ROWS = 8

N_PAIRS = N_HEADS // 2
LOG2E = math.log2(math.e)
N_SPLIT = 3
ATTN_BLOCK = 512
TOKEN_BLOCK = 256
SAMPLE_KEY_BLOCK = 1024
CUMSUM_BLOCK = 2048

STAT_BASE, STAT_BMAX, STAT_KNORM, STAT_QNORM, N_STATS = 0, 1, 2, 3, 4
STAT_ROWS = 8
NORM_SLACK = 1.01
EXP2_ZERO_BELOW = -150.0
SKIP_MARGIN = 4.0

F32 = jnp.float32
BF16 = jnp.bfloat16
U32 = jnp.uint32


def _log_sigmoid(x):
    return jnp.minimum(x, 0.0) - jnp.log1p(jnp.exp(-jnp.abs(x)))


def _gelu_tanh(x):
    c = math.sqrt(2.0 / math.pi)
    return 0.5 * x * (1.0 + jnp.tanh(c * (x + 0.044715 * (x * x * x))))


def _rmsnorm(x, g):
    return x * lax.rsqrt(jnp.mean(x * x, axis=-1, keepdims=True) + EPS) * g


def _scan_affine_rows(a, b):
    n = a.shape[0]
    row = lax.broadcasted_iota(jnp.int32, a.shape, 0)
    d = 1
    while d < n:
        keep = row >= d
        a_sh = jnp.where(keep, pltpu.roll(a, d, 0), 1.0)
        b_sh = jnp.where(keep, pltpu.roll(b, d, 0), 0.0)
        b = a * b_sh + b
        a = a * a_sh
        d *= 2
    return a, b


def _cumsum_rows(x):
    n = x.shape[0]
    row = lax.broadcasted_iota(jnp.int32, x.shape, 0)
    d = 1
    while d < n:
        x = x + jnp.where(row >= d, pltpu.roll(x, d, 0), 0.0)
        d *= 2
    return x


def _const_spec(shape):
    return pl.BlockSpec(shape, lambda *_: (0,) * len(shape), pipeline_mode=pl.Buffered(1))


def _own_lanes(lane, head):
    return (lane < HEAD_DIM) if head % 2 == 0 else (lane >= HEAD_DIM)


def _aug_lane0(head):
    return HEAD_DIM if head % 2 == 0 else 0


def _zero_from(x):
    u = pltpu.bitcast(x, U32)
    return pltpu.bitcast(lax.shift_right_logical(lax.shift_right_logical(u, U32(16)), U32(16)), F32)


def _proj_lru_kernel(x_ref, conv0_ref, h0_ref, nmix_ref, win_ref, convw_ref, convb_ref, wgate_ref,
                     brg_ref, big_ref, lam_ref, bf_ref,
                     q_ref, kb_ref, vb_ref, vt_ref, kf_ref, vf_ref, logf_ref, ylru_ref, convo_ref, hlast_ref,
                     xr_buf, h_carry, *, reset_first):
    t = pl.program_id(1)
    tb, d = x_ref.shape[1], x_ref.shape[2]
    a_dim = N_HEADS * HEAD_DIM

    xb = _rmsnorm(x_ref[0], nmix_ref[...]).astype(BF16)

    def proj(lo, width):
        return jnp.dot(xb, win_ref[:, lo:lo + width], preferred_element_type=F32)

    xr = proj(0, d)
    gate = proj(d, d)
    q = proj(2 * d, a_dim)
    k = proj(2 * d + a_dim, a_dim)
    v = proj(2 * d + 2 * a_dim, a_dim)
    f_logit = proj(2 * d + 3 * a_dim, LANES)

    qb = (q * (LOG2E * HEAD_DIM ** -0.5)).astype(BF16)
    kb, vb, vtb = k.astype(BF16), v.astype(BF16), v.T.astype(BF16)
    for hp in range(N_PAIRS):
        ls = slice(hp * LANES, (hp + 1) * LANES)
        q_ref[0, hp] = qb[:, ls]
        kb_ref[0, hp] = kb[:, ls]
        vb_ref[0, hp] = vb[:, ls]
    for h in range(N_HEADS):
        vt_ref[0, h] = vtb[h * HEAD_DIM:(h + 1) * HEAD_DIM, :]
    kf_ref[0] = k
    vf_ref[0] = v
    logf_ref[0] = _log_sigmoid(f_logit + bf_ref[...])

    lo = CARRY_ROWS - (CONV_WIDTH - 1)

    @pl.when(t == 0)
    def _():
        xr_buf[lo:CARRY_ROWS, :] = conv0_ref[0]
        h_carry[...] = h0_ref[0]

    xr_buf[CARRY_ROWS:CARRY_ROWS + tb, :] = xr
    xc = convb_ref[...] + xr_buf[lo:lo + tb, :] * convw_ref[0:1, :]
    for j in range(1, CONV_WIDTH):
        xc = xc + xr_buf[lo + j:lo + j + tb, :] * convw_ref[j:j + 1, :]
    tail = xr_buf[tb + lo:tb + CARRY_ROWS, :]
    convo_ref[0] = tail
    xr_buf[lo:CARRY_ROWS, :] = tail

    xcb = xc.astype(BF16)
    parts = [jnp.dot(xcb[:, g * MXU_DIM:(g + 1) * MXU_DIM], wgate_ref[g], preferred_element_type=F32)
             for g in range(d // MXU_DIM)]
    r = jax.nn.sigmoid(jnp.concatenate([p[:, :MXU_DIM] for p in parts], axis=1) + brg_ref[...])
    i = jax.nn.sigmoid(jnp.concatenate([p[:, MXU_DIM:] for p in parts], axis=1) + big_ref[...])
    a = jnp.exp(LRU_C * r * _log_sigmoid(lam_ref[...]))
    mult = jnp.sqrt(1.0 - a * a)
    if reset_first:
        row = lax.broadcasted_iota(jnp.int32, mult.shape, 0)
        mult = jnp.where(jnp.logical_and(row == 0, t == 0), 1.0, mult)
    a_cum, b_cum = _scan_affine_rows(a, mult * i * xc)
    h = a_cum * h_carry[...] + b_cum
    h_carry[...] = h[tb - 1:tb, :]
    hlast_ref[0] = h[tb - 1:tb, :]
    ylru_ref[0] = (h * _gelu_tanh(gate)).astype(BF16)


def _proj_lru(x, conv0, h0, w, *, reset_first):
    b, t, d = x.shape
    a_dim = N_HEADS * HEAD_DIM
    tb = min(TOKEN_BLOCK, t)
    grid = (b, t // tb)
    tok = lambda width: pl.BlockSpec((1, tb, width), lambda bi, ti: (bi, ti, 0))
    per_b = lambda rows: pl.BlockSpec((1, rows, d), lambda bi, ti: (bi, 0, 0))
    pair_major = pl.BlockSpec((1, N_PAIRS, tb, LANES), lambda bi, ti: (bi, 0, ti, 0))
    out_shapes = (
        jax.ShapeDtypeStruct((b, N_PAIRS, t, LANES), BF16),
        jax.ShapeDtypeStruct((b, N_PAIRS, t, LANES), BF16),
        jax.ShapeDtypeStruct((b, N_PAIRS, t, LANES), BF16),
        jax.ShapeDtypeStruct((b, N_HEADS, HEAD_DIM, t), BF16),
        jax.ShapeDtypeStruct((b, t, a_dim), F32),
        jax.ShapeDtypeStruct((b, t, a_dim), F32),
        jax.ShapeDtypeStruct((b, t, LANES), F32),
        jax.ShapeDtypeStruct((b, t, d), BF16),
        jax.ShapeDtypeStruct((b, CONV_WIDTH - 1, d), F32),
        jax.ShapeDtypeStruct((b, 1, d), F32),
    )
    out_specs = (
        pair_major, pair_major, pair_major,
        pl.BlockSpec((1, N_HEADS, HEAD_DIM, tb), lambda bi, ti: (bi, 0, 0, ti)),
        tok(a_dim), tok(a_dim), tok(LANES), tok(d),
        per_b(CONV_WIDTH - 1), per_b(1),
    )
    in_specs = [
        tok(d), per_b(CONV_WIDTH - 1), per_b(1),
        _const_spec(w["norm_mix"].shape), _const_spec(w["w_in1"].shape),
        _const_spec(w["conv_w"].shape), _const_spec(w["conv_b"].shape), _const_spec(w["w_gates"].shape),
        _const_spec(w["b_rg"].shape), _const_spec(w["b_ig"].shape), _const_spec(w["lam"].shape),
        _const_spec(w["b_forget"].shape),
    ]
    return pl.pallas_call(
        functools.partial(_proj_lru_kernel, reset_first=reset_first),
        grid=grid, in_specs=in_specs, out_specs=out_specs, out_shape=out_shapes,
        scratch_shapes=[pltpu.VMEM((CARRY_ROWS + tb, d), F32), pltpu.VMEM((1, d), F32)],
        compiler_params=pltpu.CompilerParams(
            dimension_semantics=("arbitrary", "arbitrary"), vmem_limit_bytes=VMEM_LIMIT_BYTES),
        name="proj_lru",
    )(x, conv0, h0, w["norm_mix"], w["w_in1"], w["conv_w"], w["conv_b"], w["w_gates"],
      w["b_rg"], w["b_ig"], w["lam"], w["b_forget"])


def _bias_placement():
    place = np.zeros((N_SPLIT * LANES, N_PAIRS * LANES), np.float32)
    for h in range(N_HEADS):
        for s in range(N_SPLIT):
            place[s * LANES + h, (h // 2) * LANES + _aug_lane0(h) + s] = 1.0
    return jnp.asarray(place, BF16)


def _head_lane_sum():
    sel = np.zeros((N_PAIRS * LANES, LANES), np.float32)
    for h in range(N_HEADS):
        lo = (h // 2) * LANES + (h % 2) * HEAD_DIM
        sel[lo:lo + HEAD_DIM, h] = 1.0
    return jnp.asarray(sel, BF16)


def _query_ones():
    ones = np.zeros((ATTN_BLOCK, LANES), np.float32)
    for h in range(2):
        ones[:, _aug_lane0(h):_aug_lane0(h) + N_SPLIT] = 1.0
    return jnp.asarray(ones, BF16)


def _max_head_norm(slabs_ref, sel_ref):
    sq = jnp.concatenate([jnp.square(slabs_ref[0, hp].astype(F32)) for hp in range(N_PAIRS)], axis=1)
    norm2 = jnp.dot(sq.astype(BF16), sel_ref[...], preferred_element_type=F32)
    return jnp.sqrt(jnp.max(norm2, axis=0, keepdims=True) * NORM_SLACK)


def _attn_prep_kernel(lf_ref, k_ref, q_ref, place_ref, sel_ref, ka_ref, st_ref, carry):
    @pl.when(pl.program_id(1) == 0)
    def _():
        carry[...] = jnp.zeros(carry.shape, F32)

    n = lf_ref.shape[1]
    c = _cumsum_rows(lf_ref[0]) + carry[...]
    carry[...] = c[n - 1:n, :]
    base = c[0:1, :]
    rest = (base - c) * LOG2E
    pieces = []
    for _ in range(N_SPLIT):
        piece = rest.astype(BF16)
        pieces.append(piece)
        rest = rest - piece.astype(F32)
    placed = jnp.dot(jnp.concatenate(pieces, axis=1), place_ref[...], preferred_element_type=F32)
    lane = lax.broadcasted_iota(jnp.int32, (n, LANES), 1)
    for h in range(N_HEADS):
        bias = placed[:, (h // 2) * LANES:(h // 2 + 1) * LANES].astype(BF16)
        ka_ref[0, h] = jnp.where(_own_lanes(lane, h), k_ref[0, h // 2], bias)

    stats = {STAT_BASE: base * LOG2E,
             STAT_BMAX: c[n - 1:n, :] * -LOG2E,
             STAT_KNORM: _max_head_norm(k_ref, sel_ref),
             STAT_QNORM: _max_head_norm(q_ref, sel_ref)}
    rows = [stats.get(r, jnp.zeros((1, LANES), F32)) for r in range(STAT_ROWS)]
    st_ref[0, 0] = jnp.concatenate(rows, axis=0)


def _attn_prep(logf, kb, q):
    b, t, _ = logf.shape
    tc = min(ATTN_BLOCK, t)
    place, sel = _bias_placement(), _head_lane_sum()
    pair_major = pl.BlockSpec((1, N_PAIRS, tc, LANES), lambda bi, ti: (bi, 0, ti, 0))
    return pl.pallas_call(
        _attn_prep_kernel,
        grid=(b, t // tc),
        in_specs=[pl.BlockSpec((1, tc, LANES), lambda bi, ti: (bi, ti, 0)), pair_major, pair_major,
                  _const_spec(place.shape), _const_spec(sel.shape)],
        out_specs=(pl.BlockSpec((1, N_HEADS, tc, LANES), lambda bi, ti: (bi, 0, ti, 0)),
                   pl.BlockSpec((1, 1, STAT_ROWS, LANES), lambda bi, ti: (bi, ti, 0, 0))),
        out_shape=(jax.ShapeDtypeStruct((b, N_HEADS, t, LANES), BF16),
                   jax.ShapeDtypeStruct((b, t // tc, STAT_ROWS, LANES), F32)),
        scratch_shapes=[pltpu.VMEM((1, LANES), F32)],
        compiler_params=pltpu.CompilerParams(dimension_semantics=("arbitrary", "arbitrary")),
        name="attn_prep",
    )(logf, kb, q, place, sel)


def _cumsum_kernel(lf_ref, off_ref, c_ref, ct_ref, carry):
    @pl.when(pl.program_id(1) == 0)
    def _():
        carry[...] = off_ref[0]

    n = lf_ref.shape[1]
    c = _cumsum_rows(lf_ref[0]) + carry[...]
    carry[...] = c[n - 1:n, :]
    c_ref[0] = c
    ct_ref[0] = c.T


def _cumsum_heads(logf, offset):
    b, t, _ = logf.shape
    tc = min(CUMSUM_BLOCK, t)
    return pl.pallas_call(
        _cumsum_kernel,
        grid=(b, t // tc),
        in_specs=[pl.BlockSpec((1, tc, LANES), lambda bi, ti: (bi, ti, 0)),
                  pl.BlockSpec((1, 1, LANES), lambda bi, ti: (bi, 0, 0))],
        out_specs=(pl.BlockSpec((1, tc, LANES), lambda bi, ti: (bi, ti, 0)),
                   pl.BlockSpec((1, LANES, tc), lambda bi, ti: (bi, 0, ti))),
        out_shape=(jax.ShapeDtypeStruct((b, t, LANES), F32), jax.ShapeDtypeStruct((b, LANES, t), F32)),
        scratch_shapes=[pltpu.VMEM((1, LANES), F32)],
        compiler_params=pltpu.CompilerParams(dimension_semantics=("arbitrary", "arbitrary")),
        name="cumsum_heads",
    )(logf, offset)


def _cumsum_lanes_kernel(lf_ref, c_ref):
    x = lf_ref[0]
    n = x.shape[1]
    lane = lax.broadcasted_iota(jnp.int32, x.shape, 1)
    d = 1
    while d < n:
        x = x + jnp.where(lane >= d, pltpu.roll(x, d, 1), 0.0)
        d *= 2
    c_ref[0] = x


def _cumsum_lanes(logf_t):
    b, h, t = logf_t.shape
    spec = pl.BlockSpec((1, h, t), lambda bi: (bi, 0, 0))
    return pl.pallas_call(
        _cumsum_lanes_kernel, grid=(b,), in_specs=[spec], out_specs=spec,
        out_shape=jax.ShapeDtypeStruct((b, h, t), F32),
        compiler_params=pltpu.CompilerParams(dimension_semantics=("parallel",)),
        name="cumsum_lanes",
    )(logf_t)


def _prompt_attn_kernel(st_ref, order_ref, q_ref, ka_ref, vt_ref, ones_ref, o_ref, qa_ref, m_ref, l_ref, acc_ref):
    bi, j, r = pl.program_id(0), pl.program_id(1), pl.program_id(2)
    nblk = pl.num_programs(1)
    kb = jnp.maximum(j - r, 0)
    tq, tk = q_ref.shape[2], ka_ref.shape[2]
    stat = lambda blk, which, h: st_ref[((bi * nblk + blk) * N_STATS + which) * N_HEADS + h]

    @pl.when(r == 0)
    def _():
        m_ref[...] = jnp.full(m_ref.shape, -jnp.inf, F32)
        l_ref[...] = jnp.zeros(l_ref.shape, F32)
        acc_ref[...] = jnp.zeros(acc_ref.shape, F32)
        lane = lax.broadcasted_iota(jnp.int32, (tq, LANES), 1)
        for h in range(N_HEADS):
            qa_ref[h] = jnp.where(_own_lanes(lane, h), q_ref[0, h // 2], ones_ref[...])

    def scores(h):
        return lax.dot_general(ka_ref[0, h], qa_ref[h], (((1,), (1,)), ((), ())), preferred_element_type=F32)

    def heads_step(heads, masked):
        if masked:
            allowed = (lax.broadcasted_iota(jnp.int32, (tk, tq), 0) <= lax.broadcasted_iota(jnp.int32, (tk, tq), 1))
        s_next = scores(heads[0])
        for n, h in enumerate(heads):
            s = s_next
            if n + 1 < len(heads):
                s_next = scores(heads[n + 1])
            if masked:
                s = jnp.where(allowed, s, -jnp.inf)
            base = stat(kb, STAT_BASE, h)
            m_old = m_ref[pl.ds(h, 1), :]
            m_new = jnp.maximum(m_old, jnp.max(s, axis=0, keepdims=True) - base)
            alpha = jnp.exp2(m_old - m_new)
            p = jnp.exp2(s - (m_new + base))
            l_ref[pl.ds(h, 1), :] = alpha * l_ref[pl.ds(h, 1), :] + jnp.sum(p, axis=0, keepdims=True)
            if n + 1 < len(heads):
                alpha = alpha + _zero_from(s_next[tk - 8:tk, tq - LANES:tq])[0:1, 0:1]
            acc_ref[h] = alpha * acc_ref[h] + jnp.dot(vt_ref[0, h], p.astype(BF16), preferred_element_type=F32)
            m_ref[pl.ds(h, 1), :] = m_new

    def is_live(h):
        bmax = stat(kb, STAT_BMAX, h)
        bound = (stat(j, STAT_QNORM, h) * stat(kb, STAT_KNORM, h) + bmax
                 + jnp.abs(bmax) * 2.0 ** -20 + SKIP_MARGIN)
        return bound - jnp.min(m_ref[pl.ds(h, 1), :]) > EXP2_ZERO_BELOW

    @pl.when(r == 0)
    def _():
        heads_step(tuple(range(N_HEADS)), True)

    @pl.when(jnp.logical_and(r > 0, r <= j))
    def _():
        heads = [order_ref[bi * N_HEADS + n] for n in range(N_HEADS)]
        live = [is_live(h) for h in heads]
        pair_live = [jnp.logical_or(live[2 * slot], live[2 * slot + 1]) for slot in range(N_PAIRS)]
        all_live = functools.reduce(jnp.logical_and, pair_live)

        @pl.when(all_live)
        def _():
            heads_step(tuple(range(N_HEADS)), False)

        for slot in range(N_PAIRS):
            @pl.when(jnp.logical_and(pair_live[slot], jnp.logical_not(all_live)))
            def _():
                heads_step((heads[2 * slot], heads[2 * slot + 1]), False)

    @pl.when(r == j)
    def _():
        inv = 1.0 / l_ref[...]
        for hp in range(N_PAIRS):
            o = jnp.concatenate([acc_ref[2 * hp + e] * inv[2 * hp + e:2 * hp + e + 1, :] for e in range(2)], axis=0)
            o_ref[0, :, hp * LANES:(hp + 1) * LANES] = o.T.astype(BF16)


def _prompt_attention(q, ka, vt, stats):
    b, _, t, _ = q.shape
    tq = min(ATTN_BLOCK, t)
    nq = t // tq
    ones = _query_ones()[:tq]
    order = jnp.argsort(stats[:, -1, STAT_BMAX, :N_HEADS], axis=-1).astype(jnp.int32)
    key_block = lambda j, r: jnp.maximum(j - r, 0)
    grid_spec = pltpu.PrefetchScalarGridSpec(
        num_scalar_prefetch=2, grid=(b, nq, nq),
        in_specs=[pl.BlockSpec((1, N_PAIRS, tq, LANES), lambda bi, j, r, *_: (bi, 0, j, 0)),
                  pl.BlockSpec((1, N_HEADS, tq, LANES), lambda bi, j, r, *_: (bi, 0, key_block(j, r), 0)),
                  pl.BlockSpec((1, N_HEADS, HEAD_DIM, tq), lambda bi, j, r, *_: (bi, 0, 0, key_block(j, r))),
                  pl.BlockSpec((tq, LANES), lambda bi, j, r, *_: (0, 0))],
        out_specs=pl.BlockSpec((1, tq, N_PAIRS * LANES), lambda bi, j, r, *_: (bi, j, 0)),
        scratch_shapes=[pltpu.VMEM((N_HEADS, tq, LANES), BF16),
                        pltpu.VMEM((N_HEADS, tq), F32), pltpu.VMEM((N_HEADS, tq), F32),
                        pltpu.VMEM((N_HEADS, HEAD_DIM, tq), F32)])
    return pl.pallas_call(
        _prompt_attn_kernel, grid_spec=grid_spec,
        out_shape=jax.ShapeDtypeStruct((b, t, N_PAIRS * LANES), BF16),
        compiler_params=pltpu.CompilerParams(
            dimension_semantics=("parallel", "parallel", "arbitrary"), vmem_limit_bytes=VMEM_LIMIT_BYTES),
        name="prompt_attention",
    )(stats[:, :, :N_STATS, :N_HEADS].reshape(-1), order.reshape(-1), q, ka, vt, ones)


def _sample_attn_kernel(q_ref, pk_ref, pv_ref, kn_ref, vn_ref, cpt_ref, cnt_ref, o_ref,
                        qw_ref, m_ref, l_ref, acc_ref, *, n_past):
    i = pl.program_id(1)
    tq = q_ref.shape[2]
    rows = 2 * tq

    @pl.when(i == 0)
    def _():
        m_ref[...] = jnp.full(m_ref.shape, -jnp.inf, F32)
        l_ref[...] = jnp.zeros(l_ref.shape, F32)
        acc_ref[...] = jnp.zeros(acc_ref.shape, F32)
        lane = lax.broadcasted_iota(jnp.int32, (tq, LANES), 1)
        for hp in range(N_PAIRS):
            pair = q_ref[0, hp]
            for e in range(2):
                qw_ref[hp, e * tq:(e + 1) * tq, :] = jnp.where(_own_lanes(lane, e), pair, jnp.zeros_like(pair))

    def step(k_of, v_of, ckt, masked, dim_major):
        tk = ckt.shape[1]
        key_axis = 1 if dim_major else 0
        contract = lambda a, b, b_axis: lax.dot_general(a, b, (((1,), (b_axis,)), ((), ())),
                                                        preferred_element_type=F32)
        s = jnp.concatenate([contract(qw_ref[hp], k_of(hp), 1 - key_axis) for hp in range(N_PAIRS)],
                            axis=0)
        ck = jnp.concatenate([jnp.broadcast_to(ckt[h:h + 1, :], (tq, tk)) for h in range(N_HEADS)], axis=0)
        s = s - ck * LOG2E
        if masked:
            q_idx = lax.rem(lax.broadcasted_iota(jnp.int32, s.shape, 0), tq)
            s = jnp.where(lax.broadcasted_iota(jnp.int32, s.shape, 1) <= q_idx, s, -jnp.inf)
        m_old = m_ref[...]
        m_new = jnp.maximum(m_old, jnp.max(s, axis=1, keepdims=True))
        alpha = jnp.exp2(m_old - m_new)
        p = jnp.exp2(s - m_new)
        l_ref[...] = alpha * l_ref[...] + jnp.sum(p, axis=1, keepdims=True)
        m_ref[...] = m_new
        pb = p.astype(BF16)
        for hp in range(N_PAIRS):
            rs = slice(hp * rows, (hp + 1) * rows)
            acc_ref[hp] = alpha[rs, :] * acc_ref[hp] + contract(pb[rs, :], v_of(hp), key_axis)

    @pl.when(i < n_past)
    def _():
        step(lambda hp: pk_ref[0, hp].astype(BF16), lambda hp: pv_ref[0, hp].astype(BF16), cpt_ref[0],
             False, True)

    @pl.when(i == n_past)
    def _():
        step(lambda hp: kn_ref[0, hp], lambda hp: vn_ref[0, hp], cnt_ref[0], True, False)
        inv = 1.0 / l_ref[...]
        for hp in range(N_PAIRS):
            o = acc_ref[hp] * inv[hp * rows:(hp + 1) * rows, :]
            for e in range(2):
                lo = hp * LANES + e * HEAD_DIM
                o_ref[0, :, lo:lo + HEAD_DIM] = o[e * tq:(e + 1) * tq, e * HEAD_DIM:(e + 1) * HEAD_DIM].astype(BF16)


def _sample_attention(q, past_kt, past_vt, kn, vn, cpt, cnt):
    b, _, tq, _ = q.shape
    p_len, a_dim = past_kt.shape[3], N_HEADS * HEAD_DIM
    tk = min(SAMPLE_KEY_BLOCK, p_len)
    n_past = p_len // tk
    past = pl.BlockSpec((1, N_PAIRS, LANES, tk), lambda bi, i: (bi, 0, 0, jnp.minimum(i, n_past - 1)))
    whole4 = pl.BlockSpec((1, N_PAIRS, tq, LANES), lambda bi, i: (bi, 0, 0, 0))
    return pl.pallas_call(
        functools.partial(_sample_attn_kernel, n_past=n_past),
        grid=(b, n_past + 1),
        in_specs=[whole4, past, past, whole4, whole4,
                  pl.BlockSpec((1, N_HEADS, tk), lambda bi, i: (bi, 0, jnp.minimum(i, n_past - 1))),
                  pl.BlockSpec((1, N_HEADS, tq), lambda bi, i: (bi, 0, 0))],
        out_specs=pl.BlockSpec((1, tq, a_dim), lambda bi, i: (bi, 0, 0)),
        out_shape=jax.ShapeDtypeStruct((b, tq, a_dim), BF16),
        scratch_shapes=[pltpu.VMEM((N_PAIRS, 2 * tq, LANES), BF16),
                        pltpu.VMEM((N_HEADS * tq, 1), F32), pltpu.VMEM((N_HEADS * tq, 1), F32),
                        pltpu.VMEM((N_PAIRS, 2 * tq, LANES), F32)],
        compiler_params=pltpu.CompilerParams(
            dimension_semantics=("parallel", "arbitrary"), vmem_limit_bytes=VMEM_LIMIT_BYTES),
        name="sample_attention",
    )(q, past_kt, past_vt, kn, vn, cpt, cnt)


def _merge_ffn_kernel(x_ref, ylru_ref, yatt_ref, nmix_ref, wg_ref, bgate_ref, wbl_ref, wba_ref, wout_ref,
                      nffn_ref, wfi_ref, wfo_ref, nfin_ref, y_ref):
    d = x_ref.shape[1]
    d_ff = wfo_ref.shape[0]
    x = x_ref[...]
    xb = _rmsnorm(x, nmix_ref[...]).astype(BF16)
    g = jax.nn.sigmoid(jnp.dot(xb, wg_ref[...], preferred_element_type=F32) + bgate_ref[...])
    mixed = (g[:, :d] * jnp.dot(ylru_ref[...], wbl_ref[...], preferred_element_type=F32)
             + g[:, d:] * jnp.dot(yatt_ref[...], wba_ref[...], preferred_element_type=F32))
    x = x + jnp.dot(mixed.astype(BF16), wout_ref[...], preferred_element_type=F32)
    xb2 = _rmsnorm(x, nffn_ref[...]).astype(BF16)
    hid = jnp.dot(xb2, wfi_ref[...], preferred_element_type=F32)
    gf, up = hid[:, :d_ff], hid[:, d_ff:]
    act = (gf * jax.nn.sigmoid(gf) * up).astype(BF16)
    x = x + jnp.dot(act, wfo_ref[...], preferred_element_type=F32)
    y_ref[...] = _rmsnorm(x, nfin_ref[...])


def _merge_ffn(x, ylru, yatt, w, norm_final):
    n, d = x.shape
    tb = min(TOKEN_BLOCK, n)
    tok = pl.BlockSpec((tb, d), lambda i: (i, 0))
    consts = [w["norm_mix"], w["w_g"], w["b_gate"], w["w_br_lru"], w["w_br_att"], w["w_out"],
              w["norm_ffn"], w["w_ffn_in"], w["w_ffn_out"], norm_final]
    return pl.pallas_call(
        _merge_ffn_kernel,
        grid=(n // tb,),
        in_specs=[tok, tok, tok] + [_const_spec(c.shape) for c in consts],
        out_specs=tok,
        out_shape=jax.ShapeDtypeStruct((n, d), F32),
        compiler_params=pltpu.CompilerParams(
            dimension_semantics=("parallel",), vmem_limit_bytes=VMEM_LIMIT_BYTES),
        name="merge_ffn",
    )(x, ylru, yatt, *consts)


def _prep_layer_weights(norm_mix, w_in, b_forget, b_gate, conv_w, conv_b, w_rg, b_rg, w_ig, b_ig, lam,
                        w_br_lru, w_br_att, w_out, norm_ffn, w_ffn_in, w_ffn_out):
    d = w_in.shape[0]
    a_dim = N_HEADS * HEAD_DIM
    n1 = 2 * d + 3 * a_dim
    row = lambda v: v.reshape(1, -1).astype(F32)
    w_f = jnp.pad(w_in[:, n1 + 2 * d:], ((0, 0), (0, LANES - N_HEADS)))
    per_group = MXU_DIM // (d // N_LRU_BLOCKS)

    def block_diag(wb):
        g = wb.reshape(-1, per_group, wb.shape[1], wb.shape[2])
        eye = jnp.eye(per_group, dtype=wb.dtype)
        return jnp.einsum("gaij,ab->gaibj", g, eye).reshape(g.shape[0], MXU_DIM, MXU_DIM)

    return {
        "norm_mix": row(norm_mix),
        "w_in1": jnp.concatenate([w_in[:, :n1], w_f], axis=1).astype(BF16),
        "w_g": w_in[:, n1:n1 + 2 * d].astype(BF16),
        "b_forget": jnp.pad(row(b_forget), ((0, 0), (0, LANES - N_HEADS))),
        "b_gate": row(b_gate),
        "conv_w": conv_w.astype(F32), "conv_b": row(conv_b),
        "w_gates": jnp.concatenate([block_diag(w_rg), block_diag(w_ig)], axis=2).astype(BF16),
        "b_rg": row(b_rg), "b_ig": row(b_ig), "lam": row(lam),
        "w_br_lru": w_br_lru.astype(BF16), "w_br_att": w_br_att.astype(BF16), "w_out": w_out.astype(BF16),
        "norm_ffn": row(norm_ffn), "w_ffn_in": w_ffn_in.astype(BF16), "w_ffn_out": w_ffn_out.astype(BF16),
    }


def _layer_prompt(x, w, norm_final):
    b, t, d = x.shape
    zeros_conv = jnp.zeros((b, CONV_WIDTH - 1, d), F32)
    zeros_h = jnp.zeros((b, 1, d), F32)
    q, kb, _, vt, kf, vf, logf, ylru, conv_o, h_last = _proj_lru(x, zeros_conv, zeros_h, w, reset_first=True)
    ka, stats = _attn_prep(logf, kb, q)
    yatt = _prompt_attention(q, ka, vt, stats)
    y = _merge_ffn(x.reshape(b * t, d), ylru.reshape(b * t, d), yatt.reshape(b * t, d), w, norm_final)
    return y.reshape(b, t, d), kf, vf, logf[..., :N_HEADS], conv_o, h_last[:, 0]


def _layer_sample(x, conv0, h0, past_k, past_v, past_logf, w, norm_final):
    b, t, d = x.shape
    p_len = past_k.shape[1]
    q, kb, vb, _, kf, vf, logf, ylru, conv_o, h_last = _proj_lru(x, conv0, h0[:, None, :], w, reset_first=False)
    dim_major = lambda kv: jnp.transpose(kv, (0, 2, 3, 1)).reshape(b, N_PAIRS, LANES, p_len)
    ct_past = _cumsum_lanes(jnp.transpose(past_logf.astype(F32), (0, 2, 1)))
    total = jnp.pad(ct_past[:, :, p_len - 1], ((0, 0), (0, LANES - N_HEADS)))[:, None, :]
    _, ct_new = _cumsum_heads(logf, total)
    yatt = _sample_attention(q, dim_major(past_k), dim_major(past_v), kb, vb, ct_past, ct_new)
    y = _merge_ffn(x.reshape(b * t, d), ylru.reshape(b * t, d), yatt.reshape(b * t, d), w, norm_final)
    return y.reshape(b, t, d), kf, vf, logf[..., :N_HEADS], conv_o, h_last[:, 0]


def kernel(x_prompt, x_sample, cache_k, cache_v, cache_logf, state_conv, state_h, norm_mix, w_in, b_forget,
           b_gate, conv_w, conv_b, w_rg, b_rg, w_ig, b_ig, lru_lambda, w_br_lru, w_br_att, w_out, norm_ffn,
           w_ffn_in, w_ffn_out, norm_final):
    assert norm_mix.shape[0] == 1, "single-layer trunk: the final rmsnorm is fused into the layer's last stage"
    heads = lambda kv: kv.reshape(kv.shape[0], kv.shape[1], N_HEADS, HEAD_DIM)[None]
    nfin = norm_final.reshape(1, -1).astype(F32)
    w = _prep_layer_weights(norm_mix[0], w_in[0], b_forget[0], b_gate[0], conv_w[0], conv_b[0], w_rg[0], b_rg[0],
                            w_ig[0], b_ig[0], lru_lambda[0], w_br_lru[0], w_br_att[0], w_out[0], norm_ffn[0],
                            w_ffn_in[0], w_ffn_out[0])
    yp, kp, vp, lfp, cp, hp = _layer_prompt(x_prompt, w, nfin)
    ys, ks, vs, lfs, cs, hs = _layer_sample(x_sample, state_conv[0], state_h[0], cache_k[0], cache_v[0],
                                            cache_logf[0], w, nfin)
    return (yp, ys, heads(kp), heads(vp), lfp[None], cp[None], hp[None],
            heads(ks), heads(vs), lfs[None], cs[None], hs[None])
```

```python
import functools
import math

import numpy as np
import jax
import jax.numpy as jnp
from jax import lax
from jax.experimental import pallas as pl
from jax.experimental.pallas import tpu as pltpu

N_HEADS = 16
HEAD_DIM = 64
N_LRU_BLOCKS = 16
CONV_WIDTH = 4
LRU_C = 8.0
EPS = 1e-6

LANES = 128
MXU_DIM = 256
VMEM_LIMIT_BYTES = 56 * 1024 * 1024
CARRY_ROWS = 8

N_PAIRS = N_HEADS // 2
LOG2E = math.log2(math.e)
N_SPLIT = 3
ATTN_BLOCK = 512
ATTN_Q_BLOCK = 512
TOKEN_BLOCK = 256
SAMPLE_KEY_BLOCK = 1024
CUMSUM_BLOCK = 2048

STAT_BASE, STAT_BMAX, STAT_KNORM, STAT_QNORM, N_STATS = 0, 1, 2, 3, 4
STAT_ROWS = 8
NORM_SLACK = 1.01
EXP2_ZERO_BELOW = -150.0
SKIP_MARGIN = 4.0

F32 = jnp.float32
BF16 = jnp.bfloat16
U32 = jnp.uint32


def _log_sigmoid(x):
    return jnp.minimum(x, 0.0) - jnp.log1p(jnp.exp(-jnp.abs(x)))


def _gelu_tanh(x):
    c = math.sqrt(2.0 / math.pi)
    return 0.5 * x * (1.0 + jnp.tanh(c * (x + 0.044715 * (x * x * x))))


def _rmsnorm(x, g):
    return x * lax.rsqrt(jnp.mean(x * x, axis=-1, keepdims=True) + EPS) * g


def _scan_affine_rows(a, b):
    n = a.shape[0]
    row = lax.broadcasted_iota(jnp.int32, a.shape, 0)
    d = 1
    while d < n:
        keep = row >= d
        a_sh = jnp.where(keep, pltpu.roll(a, d, 0), 1.0)
        b_sh = jnp.where(keep, pltpu.roll(b, d, 0), 0.0)
        b = a * b_sh + b
        a = a * a_sh
        d *= 2
    return a, b


def _cumsum_rows(x):
    n = x.shape[0]
    row = lax.broadcasted_iota(jnp.int32, x.shape, 0)
    d = 1
    while d < n:
        x = x + jnp.where(row >= d, pltpu.roll(x, d, 0), 0.0)
        d *= 2
    return x


def _const_spec(shape):
    return pl.BlockSpec(shape, lambda *_: (0,) * len(shape), pipeline_mode=pl.Buffered(1))


def _own_lanes(lane, head):
    return (lane < HEAD_DIM) if head % 2 == 0 else (lane >= HEAD_DIM)


def _aug_lane0(head):
    return HEAD_DIM if head % 2 == 0 else 0


def _zero_from(x):
    u = pltpu.bitcast(x, U32)
    return pltpu.bitcast(lax.shift_right_logical(lax.shift_right_logical(u, U32(16)), U32(16)), F32)


def _proj_lru_kernel(x_ref, conv0_ref, h0_ref, nmix_ref, win_ref, convw_ref, convb_ref, wgate_ref,
                     brg_ref, big_ref, lam_ref, bf_ref,
                     q_ref, kb_ref, vb_ref, vt_ref, kf_ref, vf_ref, logf_ref, ylru_ref, convo_ref, hlast_ref,
                     xr_buf, h_carry, *, reset_first):
    t = pl.program_id(1)
    tb, d = x_ref.shape[1], x_ref.shape[2]
    a_dim = N_HEADS * HEAD_DIM

    xb = _rmsnorm(x_ref[0], nmix_ref[...]).astype(BF16)

    def proj(lo, width):
        return jnp.dot(xb, win_ref[:, lo:lo + width], preferred_element_type=F32)

    xr = proj(0, d)
    gate = proj(d, d)
    q = proj(2 * d, a_dim)
    k = proj(2 * d + a_dim, a_dim)
    v = proj(2 * d + 2 * a_dim, a_dim)
    f_logit = proj(2 * d + 3 * a_dim, LANES)

    qb = (q * (LOG2E * HEAD_DIM ** -0.5)).astype(BF16)
    kb, vb, vtb = k.astype(BF16), v.astype(BF16), v.T.astype(BF16)
    for hp in range(N_PAIRS):
        ls = slice(hp * LANES, (hp + 1) * LANES)
        q_ref[0, hp] = qb[:, ls]
        kb_ref[0, hp] = kb[:, ls]
        vb_ref[0, hp] = vb[:, ls]
    for h in range(N_HEADS):
        vt_ref[0, h] = vtb[h * HEAD_DIM:(h + 1) * HEAD_DIM, :]
    kf_ref[0] = k
    vf_ref[0] = v
    logf_ref[0] = _log_sigmoid(f_logit + bf_ref[...])

    lo = CARRY_ROWS - (CONV_WIDTH - 1)

    @pl.when(t == 0)
    def _():
        xr_buf[lo:CARRY_ROWS, :] = conv0_ref[0]
        h_carry[...] = h0_ref[0]

    xr_buf[CARRY_ROWS:CARRY_ROWS + tb, :] = xr
    xc = convb_ref[...] + xr_buf[lo:lo + tb, :] * convw_ref[0:1, :]
    for j in range(1, CONV_WIDTH):
        xc = xc + xr_buf[lo + j:lo + j + tb, :] * convw_ref[j:j + 1, :]
    tail = xr_buf[tb + lo:tb + CARRY_ROWS, :]
    convo_ref[0] = tail
    xr_buf[lo:CARRY_ROWS, :] = tail

    xcb = xc.astype(BF16)
    parts = [jnp.dot(xcb[:, g * MXU_DIM:(g + 1) * MXU_DIM], wgate_ref[g], preferred_element_type=F32)
             for g in range(d // MXU_DIM)]
    r = jax.nn.sigmoid(jnp.concatenate([p[:, :MXU_DIM] for p in parts], axis=1) + brg_ref[...])
    i = jax.nn.sigmoid(jnp.concatenate([p[:, MXU_DIM:] for p in parts], axis=1) + big_ref[...])
    a = jnp.exp(LRU_C * r * _log_sigmoid(lam_ref[...]))
    mult = jnp.sqrt(1.0 - a * a)
    if reset_first:
        row = lax.broadcasted_iota(jnp.int32, mult.shape, 0)
        mult = jnp.where(jnp.logical_and(row == 0, t == 0), 1.0, mult)
    a_cum, b_cum = _scan_affine_rows(a, mult * i * xc)
    h = a_cum * h_carry[...] + b_cum
    h_carry[...] = h[tb - 1:tb, :]
    hlast_ref[0] = h[tb - 1:tb, :]
    ylru_ref[0] = (h * _gelu_tanh(gate)).astype(BF16)


def _proj_lru(x, conv0, h0, w, *, reset_first):
    b, t, d = x.shape
    a_dim = N_HEADS * HEAD_DIM
    tb = min(TOKEN_BLOCK, t)
    grid = (b, t // tb)
    tok = lambda width: pl.BlockSpec((1, tb, width), lambda bi, ti: (bi, ti, 0))
    per_b = lambda rows: pl.BlockSpec((1, rows, d), lambda bi, ti: (bi, 0, 0))
    pair_major = pl.BlockSpec((1, N_PAIRS, tb, LANES), lambda bi, ti: (bi, 0, ti, 0))
    out_shapes = (
        jax.ShapeDtypeStruct((b, N_PAIRS, t, LANES), BF16),
        jax.ShapeDtypeStruct((b, N_PAIRS, t, LANES), BF16),
        jax.ShapeDtypeStruct((b, N_PAIRS, t, LANES), BF16),
        jax.ShapeDtypeStruct((b, N_HEADS, HEAD_DIM, t), BF16),
        jax.ShapeDtypeStruct((b, t, a_dim), F32),
        jax.ShapeDtypeStruct((b, t, a_dim), F32),
        jax.ShapeDtypeStruct((b, t, LANES), F32),
        jax.ShapeDtypeStruct((b, t, d), BF16),
        jax.ShapeDtypeStruct((b, CONV_WIDTH - 1, d), F32),
        jax.ShapeDtypeStruct((b, 1, d), F32),
    )
    out_specs = (
        pair_major, pair_major, pair_major,
        pl.BlockSpec((1, N_HEADS, HEAD_DIM, tb), lambda bi, ti: (bi, 0, 0, ti)),
        tok(a_dim), tok(a_dim), tok(LANES), tok(d),
        per_b(CONV_WIDTH - 1), per_b(1),
    )
    in_specs = [
        tok(d), per_b(CONV_WIDTH - 1), per_b(1),
        _const_spec(w["norm_mix"].shape), _const_spec(w["w_in1"].shape),
        _const_spec(w["conv_w"].shape), _const_spec(w["conv_b"].shape), _const_spec(w["w_gates"].shape),
        _const_spec(w["b_rg"].shape), _const_spec(w["b_ig"].shape), _const_spec(w["lam"].shape),
        _const_spec(w["b_forget"].shape),
    ]
    return pl.pallas_call(
        functools.partial(_proj_lru_kernel, reset_first=reset_first),
        grid=grid, in_specs=in_specs, out_specs=out_specs, out_shape=out_shapes,
        scratch_shapes=[pltpu.VMEM((CARRY_ROWS + tb, d), F32), pltpu.VMEM((1, d), F32)],
        compiler_params=pltpu.CompilerParams(
            dimension_semantics=("arbitrary", "arbitrary"), vmem_limit_bytes=VMEM_LIMIT_BYTES),
        name="proj_lru",
    )(x, conv0, h0, w["norm_mix"], w["w_in1"], w["conv_w"], w["conv_b"], w["w_gates"],
      w["b_rg"], w["b_ig"], w["lam"], w["b_forget"])


def _bias_placement():
    place = np.zeros((N_SPLIT * LANES, N_PAIRS * LANES), np.float32)
    for h in range(N_HEADS):
        for s in range(N_SPLIT):
            place[s * LANES + h, (h // 2) * LANES + _aug_lane0(h) + s] = 1.0
    return jnp.asarray(place, BF16)


def _head_lane_sum():
    sel = np.zeros((N_PAIRS * LANES, LANES), np.float32)
    for h in range(N_HEADS):
        lo = (h // 2) * LANES + (h % 2) * HEAD_DIM
        sel[lo:lo + HEAD_DIM, h] = 1.0
    return jnp.asarray(sel, BF16)


def _query_ones(rows):
    ones = np.zeros((rows, LANES), np.float32)
    for h in range(2):
        ones[:, _aug_lane0(h):_aug_lane0(h) + N_SPLIT] = 1.0
    return jnp.asarray(ones, BF16)


def _max_head_norm(slabs_ref, sel_ref):
    sq = jnp.concatenate([jnp.square(slabs_ref[0, hp].astype(F32)) for hp in range(N_PAIRS)], axis=1)
    norm2 = jnp.dot(sq.astype(BF16), sel_ref[...], preferred_element_type=F32)
    return jnp.sqrt(jnp.max(norm2, axis=0, keepdims=True) * NORM_SLACK)


def _attn_prep_kernel(lf_ref, k_ref, q_ref, place_ref, sel_ref, ka_ref, st_ref, carry):
    @pl.when(pl.program_id(1) == 0)
    def _():
        carry[...] = jnp.zeros(carry.shape, F32)

    n = lf_ref.shape[1]
    c = _cumsum_rows(lf_ref[0]) + carry[...]
    carry[...] = c[n - 1:n, :]
    base = c[0:1, :]
    rest = (base - c) * LOG2E
    pieces = []
    for _ in range(N_SPLIT):
        piece = rest.astype(BF16)
        pieces.append(piece)
        rest = rest - piece.astype(F32)
    placed = jnp.dot(jnp.concatenate(pieces, axis=1), place_ref[...], preferred_element_type=F32)
    lane = lax.broadcasted_iota(jnp.int32, (n, LANES), 1)
    for h in range(N_HEADS):
        bias = placed[:, (h // 2) * LANES:(h // 2 + 1) * LANES].astype(BF16)
        ka_ref[0, h] = jnp.where(_own_lanes(lane, h), k_ref[0, h // 2], bias)

    stats = {STAT_BASE: base * LOG2E,
             STAT_BMAX: c[n - 1:n, :] * -LOG2E,
             STAT_KNORM: _max_head_norm(k_ref, sel_ref),
             STAT_QNORM: _max_head_norm(q_ref, sel_ref)}
    rows = [stats.get(r, jnp.zeros((1, LANES), F32)) for r in range(STAT_ROWS)]
    st_ref[0, 0] = jnp.concatenate(rows, axis=0)


def _attn_prep(logf, kb, q):
    b, t, _ = logf.shape
    tc = min(ATTN_BLOCK, t)
    place, sel = _bias_placement(), _head_lane_sum()
    pair_major = pl.BlockSpec((1, N_PAIRS, tc, LANES), lambda bi, ti: (bi, 0, ti, 0))
    return pl.pallas_call(
        _attn_prep_kernel,
        grid=(b, t // tc),
        in_specs=[pl.BlockSpec((1, tc, LANES), lambda bi, ti: (bi, ti, 0)), pair_major, pair_major,
                  _const_spec(place.shape), _const_spec(sel.shape)],
        out_specs=(pl.BlockSpec((1, N_HEADS, tc, LANES), lambda bi, ti: (bi, 0, ti, 0)),
                   pl.BlockSpec((1, 1, STAT_ROWS, LANES), lambda bi, ti: (bi, ti, 0, 0))),
        out_shape=(jax.ShapeDtypeStruct((b, N_HEADS, t, LANES), BF16),
                   jax.ShapeDtypeStruct((b, t // tc, STAT_ROWS, LANES), F32)),
        scratch_shapes=[pltpu.VMEM((1, LANES), F32)],
        compiler_params=pltpu.CompilerParams(dimension_semantics=("arbitrary", "arbitrary")),
        name="attn_prep",
    )(logf, kb, q, place, sel)


def _cumsum_kernel(lf_ref, off_ref, c_ref, ct_ref, carry):
    @pl.when(pl.program_id(1) == 0)
    def _():
        carry[...] = off_ref[0]

    n = lf_ref.shape[1]
    c = _cumsum_rows(lf_ref[0]) + carry[...]
    carry[...] = c[n - 1:n, :]
    c_ref[0] = c
    ct_ref[0] = c.T


def _cumsum_heads(logf, offset):
    b, t, _ = logf.shape
    tc = min(CUMSUM_BLOCK, t)
    return pl.pallas_call(
        _cumsum_kernel,
        grid=(b, t // tc),
        in_specs=[pl.BlockSpec((1, tc, LANES), lambda bi, ti: (bi, ti, 0)),
                  pl.BlockSpec((1, 1, LANES), lambda bi, ti: (bi, 0, 0))],
        out_specs=(pl.BlockSpec((1, tc, LANES), lambda bi, ti: (bi, ti, 0)),
                   pl.BlockSpec((1, LANES, tc), lambda bi, ti: (bi, 0, ti))),
        out_shape=(jax.ShapeDtypeStruct((b, t, LANES), F32), jax.ShapeDtypeStruct((b, LANES, t), F32)),
        scratch_shapes=[pltpu.VMEM((1, LANES), F32)],
        compiler_params=pltpu.CompilerParams(dimension_semantics=("arbitrary", "arbitrary")),
        name="cumsum_heads",
    )(logf, offset)


def _cumsum_lanes_kernel(lf_ref, c_ref):
    x = lf_ref[0]
    n = x.shape[1]
    lane = lax.broadcasted_iota(jnp.int32, x.shape, 1)
    d = 1
    while d < n:
        x = x + jnp.where(lane >= d, pltpu.roll(x, d, 1), 0.0)
        d *= 2
    c_ref[0] = x


def _cumsum_lanes(logf_t):
    b, h, t = logf_t.shape
    spec = pl.BlockSpec((1, h, t), lambda bi: (bi, 0, 0))
    return pl.pallas_call(
        _cumsum_lanes_kernel, grid=(b,), in_specs=[spec], out_specs=spec,
        out_shape=jax.ShapeDtypeStruct((b, h, t), F32),
        compiler_params=pltpu.CompilerParams(dimension_semantics=("parallel",)),
        name="cumsum_lanes",
    )(logf_t)


def _prompt_attn_kernel(st_ref, order_ref, q_ref, ka_ref, vt_ref, ones_ref, o_ref, qa_ref, m_ref, l_ref, acc_ref):
    bi, j, r = pl.program_id(0), pl.program_id(1), pl.program_id(2)
    nblk = pl.num_programs(2)
    tq, tk = q_ref.shape[2], ka_ref.shape[2]
    ratio = tq // tk
    last = ratio * (j + 1) - 1
    diag = r < ratio
    kb = jnp.where(diag, ratio * j + r, jnp.maximum(last - r, 0))
    stat = lambda blk, which, h: st_ref[((bi * nblk + blk) * N_STATS + which) * N_HEADS + h]

    @pl.when(r == 0)
    def _():
        m_ref[...] = jnp.full(m_ref.shape, -jnp.inf, F32)
        l_ref[...] = jnp.zeros(l_ref.shape, F32)
        acc_ref[...] = jnp.zeros(acc_ref.shape, F32)
        lane = lax.broadcasted_iota(jnp.int32, (tq, LANES), 1)
        for h in range(N_HEADS):
            qa_ref[h] = jnp.where(_own_lanes(lane, h), q_ref[0, h // 2], ones_ref[...])

    def scores(h):
        return lax.dot_general(ka_ref[0, h], qa_ref[h], (((1,), (1,)), ((), ())), preferred_element_type=F32)

    def heads_step(heads, masked):
        if masked:
            first_key = r * tk if ratio > 1 else 0
            allowed = (lax.broadcasted_iota(jnp.int32, (tk, tq), 0) + first_key
                       <= lax.broadcasted_iota(jnp.int32, (tk, tq), 1))
        s_next = scores(heads[0])
        for n, h in enumerate(heads):
            s = s_next
            if n + 1 < len(heads):
                s_next = scores(heads[n + 1])
            if masked:
                s = jnp.where(allowed, s, -jnp.inf)
            base = stat(kb, STAT_BASE, h)
            m_old = m_ref[pl.ds(h, 1), :]
            m_new = jnp.maximum(m_old, jnp.max(s, axis=0, keepdims=True) - base)
            alpha = jnp.exp2(m_old - m_new)
            p = jnp.exp2(s - (m_new + base))
            l_ref[pl.ds(h, 1), :] = alpha * l_ref[pl.ds(h, 1), :] + jnp.sum(p, axis=0, keepdims=True)
            if n + 1 < len(heads):
                alpha = alpha + _zero_from(s_next[tk - 8:tk, tq - LANES:tq])[0:1, 0:1]
            acc_ref[h] = alpha * acc_ref[h] + jnp.dot(vt_ref[0, h], p.astype(BF16), preferred_element_type=F32)
            m_ref[pl.ds(h, 1), :] = m_new

    def is_live(h):
        bmax = stat(kb, STAT_BMAX, h)
        qnorm = functools.reduce(jnp.maximum, [stat(ratio * j + n, STAT_QNORM, h) for n in range(ratio)])
        bound = qnorm * stat(kb, STAT_KNORM, h) + bmax + jnp.abs(bmax) * 2.0 ** -20 + SKIP_MARGIN
        return bound - jnp.min(m_ref[pl.ds(h, 1), :]) > EXP2_ZERO_BELOW

    @pl.when(diag)
    def _():
        heads_step(tuple(range(N_HEADS)), True)

    @pl.when(jnp.logical_and(r >= ratio, r <= last))
    def _():
        heads = [order_ref[bi * N_HEADS + n] for n in range(N_HEADS)]
        live = [is_live(h) for h in heads]
        pair_live = [jnp.logical_or(live[2 * slot], live[2 * slot + 1]) for slot in range(N_PAIRS)]

        def run(lo, n):
            if n == 1:
                @pl.when(pair_live[lo])
                def _():
                    heads_step((heads[2 * lo], heads[2 * lo + 1]), False)
                return
            whole = functools.reduce(jnp.logical_and, pair_live[lo:lo + n])

            @pl.when(whole)
            def _():
                heads_step(tuple(range(N_HEADS)) if n == N_PAIRS else tuple(heads[2 * lo:2 * (lo + n)]), False)

            @pl.when(jnp.logical_not(whole))
            def _():
                run(lo, n // 2)
                run(lo + n // 2, n // 2)

        run(0, N_PAIRS)

    @pl.when(r == last)
    def _():
        inv = 1.0 / l_ref[...]
        for hp in range(N_PAIRS):
            o = jnp.concatenate([acc_ref[2 * hp + e] * inv[2 * hp + e:2 * hp + e + 1, :] for e in range(2)], axis=0)
            o_ref[0, :, hp * LANES:(hp + 1) * LANES] = o.T.astype(BF16)


def _prompt_attention(q, ka, vt, stats):
    b, _, t, _ = q.shape
    tk = min(ATTN_BLOCK, t)
    tq = min(ATTN_Q_BLOCK, t)
    ratio = tq // tk
    nq = t // tq
    ones = _query_ones(tq)
    order = jnp.argsort(stats[:, -1, STAT_BMAX, :N_HEADS], axis=-1).astype(jnp.int32)
    key_block = lambda j, r: jnp.where(r < ratio, ratio * j + r, jnp.maximum(ratio * (j + 1) - 1 - r, 0))
    grid_spec = pltpu.PrefetchScalarGridSpec(
        num_scalar_prefetch=2, grid=(b, nq, t // tk),
        in_specs=[pl.BlockSpec((1, N_PAIRS, tq, LANES), lambda bi, j, r, *_: (bi, 0, j, 0)),
                  pl.BlockSpec((1, N_HEADS, tk, LANES), lambda bi, j, r, *_: (bi, 0, key_block(j, r), 0)),
                  pl.BlockSpec((1, N_HEADS, HEAD_DIM, tk), lambda bi, j, r, *_: (bi, 0, 0, key_block(j, r))),
                  pl.BlockSpec((tq, LANES), lambda bi, j, r, *_: (0, 0))],
        out_specs=pl.BlockSpec((1, tq, N_PAIRS * LANES), lambda bi, j, r, *_: (bi, j, 0)),
        scratch_shapes=[pltpu.VMEM((N_HEADS, tq, LANES), BF16),
                        pltpu.VMEM((N_HEADS, tq), F32), pltpu.VMEM((N_HEADS, tq), F32),
                        pltpu.VMEM((N_HEADS, HEAD_DIM, tq), F32)])
    return pl.pallas_call(
        _prompt_attn_kernel, grid_spec=grid_spec,
        out_shape=jax.ShapeDtypeStruct((b, t, N_PAIRS * LANES), BF16),
        compiler_params=pltpu.CompilerParams(
            dimension_semantics=("parallel", "parallel", "arbitrary"), vmem_limit_bytes=VMEM_LIMIT_BYTES),
        name="prompt_attention",
    )(stats[:, :, :N_STATS, :N_HEADS].reshape(-1), order.reshape(-1), q, ka, vt, ones)


def _sample_attn_kernel(q_ref, pk_ref, pv_ref, kn_ref, vn_ref, cpt_ref, cnt_ref, o_ref,
                        qw_ref, m_ref, l_ref, acc_ref, *, n_past):
    i = pl.program_id(1)
    tq = q_ref.shape[2]
    rows = 2 * tq

    @pl.when(i == 0)
    def _():
        m_ref[...] = jnp.full(m_ref.shape, -jnp.inf, F32)
        l_ref[...] = jnp.zeros(l_ref.shape, F32)
        acc_ref[...] = jnp.zeros(acc_ref.shape, F32)
        lane = lax.broadcasted_iota(jnp.int32, (tq, LANES), 1)
        for hp in range(N_PAIRS):
            pair = q_ref[0, hp]
            for e in range(2):
                qw_ref[hp, e * tq:(e + 1) * tq, :] = jnp.where(_own_lanes(lane, e), pair, jnp.zeros_like(pair))

    def step(k_of, v_of, ckt, masked, dim_major):
        tk = ckt.shape[1]
        key_axis = 1 if dim_major else 0
        contract = lambda a, b, b_axis: lax.dot_general(a, b, (((1,), (b_axis,)), ((), ())),
                                                        preferred_element_type=F32)
        s = jnp.concatenate([contract(qw_ref[hp], k_of(hp), 1 - key_axis) for hp in range(N_PAIRS)],
                            axis=0)
        ck = jnp.concatenate([jnp.broadcast_to(ckt[h:h + 1, :], (tq, tk)) for h in range(N_HEADS)], axis=0)
        s = s - ck * LOG2E
        if masked:
            q_idx = lax.rem(lax.broadcasted_iota(jnp.int32, s.shape, 0), tq)
            s = jnp.where(lax.broadcasted_iota(jnp.int32, s.shape, 1) <= q_idx, s, -jnp.inf)
        m_old = m_ref[...]
        m_new = jnp.maximum(m_old, jnp.max(s, axis=1, keepdims=True))
        alpha = jnp.exp2(m_old - m_new)
        p = jnp.exp2(s - m_new)
        l_ref[...] = alpha * l_ref[...] + jnp.sum(p, axis=1, keepdims=True)
        m_ref[...] = m_new
        pb = p.astype(BF16)
        for hp in range(N_PAIRS):
            rs = slice(hp * rows, (hp + 1) * rows)
            acc_ref[hp] = alpha[rs, :] * acc_ref[hp] + contract(pb[rs, :], v_of(hp), key_axis)

    @pl.when(i < n_past)
    def _():
        step(lambda hp: pk_ref[0, hp].astype(BF16), lambda hp: pv_ref[0, hp].astype(BF16), cpt_ref[0],
             False, True)

    @pl.when(i == n_past)
    def _():
        step(lambda hp: kn_ref[0, hp], lambda hp: vn_ref[0, hp], cnt_ref[0], True, False)
        inv = 1.0 / l_ref[...]
        for hp in range(N_PAIRS):
            o = acc_ref[hp] * inv[hp * rows:(hp + 1) * rows, :]
            for e in range(2):
                lo = hp * LANES + e * HEAD_DIM
                o_ref[0, :, lo:lo + HEAD_DIM] = o[e * tq:(e + 1) * tq, e * HEAD_DIM:(e + 1) * HEAD_DIM].astype(BF16)


def _sample_attention(q, past_kt, past_vt, kn, vn, cpt, cnt):
    b, _, tq, _ = q.shape
    p_len, a_dim = past_kt.shape[3], N_HEADS * HEAD_DIM
    tk = min(SAMPLE_KEY_BLOCK, p_len)
    n_past = p_len // tk
    past = pl.BlockSpec((1, N_PAIRS, LANES, tk), lambda bi, i: (bi, 0, 0, jnp.minimum(i, n_past - 1)))
    whole4 = pl.BlockSpec((1, N_PAIRS, tq, LANES), lambda bi, i: (bi, 0, 0, 0))
    return pl.pallas_call(
        functools.partial(_sample_attn_kernel, n_past=n_past),
        grid=(b, n_past + 1),
        in_specs=[whole4, past, past, whole4, whole4,
                  pl.BlockSpec((1, N_HEADS, tk), lambda bi, i: (bi, 0, jnp.minimum(i, n_past - 1))),
                  pl.BlockSpec((1, N_HEADS, tq), lambda bi, i: (bi, 0, 0))],
        out_specs=pl.BlockSpec((1, tq, a_dim), lambda bi, i: (bi, 0, 0)),
        out_shape=jax.ShapeDtypeStruct((b, tq, a_dim), BF16),
        scratch_shapes=[pltpu.VMEM((N_PAIRS, 2 * tq, LANES), BF16),
                        pltpu.VMEM((N_HEADS * tq, 1), F32), pltpu.VMEM((N_HEADS * tq, 1), F32),
                        pltpu.VMEM((N_PAIRS, 2 * tq, LANES), F32)],
        compiler_params=pltpu.CompilerParams(
            dimension_semantics=("parallel", "arbitrary"), vmem_limit_bytes=VMEM_LIMIT_BYTES),
        name="sample_attention",
    )(q, past_kt, past_vt, kn, vn, cpt, cnt)


def _merge_ffn_kernel(x_ref, ylru_ref, yatt_ref, nmix_ref, wg_ref, bgate_ref, wbl_ref, wba_ref, wout_ref,
                      nffn_ref, wfi_ref, wfo_ref, nfin_ref, y_ref):
    d = x_ref.shape[1]
    d_ff = wfo_ref.shape[0]
    x = x_ref[...]
    xb = _rmsnorm(x, nmix_ref[...]).astype(BF16)
    g = jax.nn.sigmoid(jnp.dot(xb, wg_ref[...], preferred_element_type=F32) + bgate_ref[...])
    mixed = (g[:, :d] * jnp.dot(ylru_ref[...], wbl_ref[...], preferred_element_type=F32)
             + g[:, d:] * jnp.dot(yatt_ref[...], wba_ref[...], preferred_element_type=F32))
    x = x + jnp.dot(mixed.astype(BF16), wout_ref[...], preferred_element_type=F32)
    xb2 = _rmsnorm(x, nffn_ref[...]).astype(BF16)
    hid = jnp.dot(xb2, wfi_ref[...], preferred_element_type=F32)
    gf, up = hid[:, :d_ff], hid[:, d_ff:]
    act = (gf * jax.nn.sigmoid(gf) * up).astype(BF16)
    x = x + jnp.dot(act, wfo_ref[...], preferred_element_type=F32)
    y_ref[...] = _rmsnorm(x, nfin_ref[...])


def _merge_ffn(x, ylru, yatt, w, norm_final):
    n, d = x.shape
    tb = min(TOKEN_BLOCK, n)
    tok = pl.BlockSpec((tb, d), lambda i: (i, 0))
    consts = [w["norm_mix"], w["w_g"], w["b_gate"], w["w_br_lru"], w["w_br_att"], w["w_out"],
              w["norm_ffn"], w["w_ffn_in"], w["w_ffn_out"], norm_final]
    return pl.pallas_call(
        _merge_ffn_kernel,
        grid=(n // tb,),
        in_specs=[tok, tok, tok] + [_const_spec(c.shape) for c in consts],
        out_specs=tok,
        out_shape=jax.ShapeDtypeStruct((n, d), F32),
        compiler_params=pltpu.CompilerParams(
            dimension_semantics=("parallel",), vmem_limit_bytes=VMEM_LIMIT_BYTES),
        name="merge_ffn",
    )(x, ylru, yatt, *consts)


def _prep_layer_weights(norm_mix, w_in, b_forget, b_gate, conv_w, conv_b, w_rg, b_rg, w_ig, b_ig, lam,
                        w_br_lru, w_br_att, w_out, norm_ffn, w_ffn_in, w_ffn_out):
    d = w_in.shape[0]
    a_dim = N_HEADS * HEAD_DIM
    n1 = 2 * d + 3 * a_dim
    row = lambda v: v.reshape(1, -1).astype(F32)
    w_f = jnp.pad(w_in[:, n1 + 2 * d:], ((0, 0), (0, LANES - N_HEADS)))
    per_group = MXU_DIM // (d // N_LRU_BLOCKS)

    def block_diag(wb):
        g = wb.reshape(-1, per_group, wb.shape[1], wb.shape[2])
        eye = jnp.eye(per_group, dtype=wb.dtype)
        return jnp.einsum("gaij,ab->gaibj", g, eye).reshape(g.shape[0], MXU_DIM, MXU_DIM)

    return {
        "norm_mix": row(norm_mix),
        "w_in1": jnp.concatenate([w_in[:, :n1], w_f], axis=1).astype(BF16),
        "w_g": w_in[:, n1:n1 + 2 * d].astype(BF16),
        "b_forget": jnp.pad(row(b_forget), ((0, 0), (0, LANES - N_HEADS))),
        "b_gate": row(b_gate),
        "conv_w": conv_w.astype(F32), "conv_b": row(conv_b),
        "w_gates": jnp.concatenate([block_diag(w_rg), block_diag(w_ig)], axis=2).astype(BF16),
        "b_rg": row(b_rg), "b_ig": row(b_ig), "lam": row(lam),
        "w_br_lru": w_br_lru.astype(BF16), "w_br_att": w_br_att.astype(BF16), "w_out": w_out.astype(BF16),
        "norm_ffn": row(norm_ffn), "w_ffn_in": w_ffn_in.astype(BF16), "w_ffn_out": w_ffn_out.astype(BF16),
    }


def _layer_prompt(x, w, norm_final):
    b, t, d = x.shape
    zeros_conv = jnp.zeros((b, CONV_WIDTH - 1, d), F32)
    zeros_h = jnp.zeros((b, 1, d), F32)
    q, kb, _, vt, kf, vf, logf, ylru, conv_o, h_last = _proj_lru(x, zeros_conv, zeros_h, w, reset_first=True)
    ka, stats = _attn_prep(logf, kb, q)
    yatt = _prompt_attention(q, ka, vt, stats)
    y = _merge_ffn(x.reshape(b * t, d), ylru.reshape(b * t, d), yatt.reshape(b * t, d), w, norm_final)
    return y.reshape(b, t, d), kf, vf, logf[..., :N_HEADS], conv_o, h_last[:, 0]


def _layer_sample(x, conv0, h0, past_k, past_v, past_logf, w, norm_final):
    b, t, d = x.shape
    p_len = past_k.shape[1]
    q, kb, vb, _, kf, vf, logf, ylru, conv_o, h_last = _proj_lru(x, conv0, h0[:, None, :], w, reset_first=False)
    dim_major = lambda kv: jnp.transpose(kv, (0, 2, 3, 1)).reshape(b, N_PAIRS, LANES, p_len)
    ct_past = _cumsum_lanes(jnp.transpose(past_logf.astype(F32), (0, 2, 1)))
    total = jnp.pad(ct_past[:, :, p_len - 1], ((0, 0), (0, LANES - N_HEADS)))[:, None, :]
    _, ct_new = _cumsum_heads(logf, total)
    yatt = _sample_attention(q, dim_major(past_k), dim_major(past_v), kb, vb, ct_past, ct_new)
    y = _merge_ffn(x.reshape(b * t, d), ylru.reshape(b * t, d), yatt.reshape(b * t, d), w, norm_final)
    return y.reshape(b, t, d), kf, vf, logf[..., :N_HEADS], conv_o, h_last[:, 0]


def kernel(x_prompt, x_sample, cache_k, cache_v, cache_logf, state_conv, state_h, norm_mix, w_in, b_forget,
           b_gate, conv_w, conv_b, w_rg, b_rg, w_ig, b_ig, lru_lambda, w_br_lru, w_br_att, w_out, norm_ffn,
           w_ffn_in, w_ffn_out, norm_final):
    assert norm_mix.shape[0] == 1, "single-layer trunk: the final rmsnorm is fused into the layer's last stage"
    heads = lambda kv: kv.reshape(kv.shape[0], kv.shape[1], N_HEADS, HEAD_DIM)[None]
    nfin = norm_final.reshape(1, -1).astype(F32)
    w = _prep_layer_weights(norm_mix[0], w_in[0], b_forget[0], b_gate[0], conv_w[0], conv_b[0], w_rg[0], b_rg[0],
                            w_ig[0], b_ig[0], lru_lambda[0], w_br_lru[0], w_br_att[0], w_out[0], norm_ffn[0],
                            w_ffn_in[0], w_ffn_out[0])
    yp, kp, vp, lfp, cp, hp = _layer_prompt(x_prompt, w, nfin)
    ys, ks, vs, lfs, cs, hs = _layer_sample(x_sample, state_conv[0], state_h[0], cache_k[0], cache_v[0],
                                            cache_logf[0], w, nfin)
    return (yp, ys, heads(kp), heads(vp), lfp[None], cp[None], hp[None],
            heads(ks), heads(vs), lfs[None], cs[None], hs[None])
```

```python
import functools
import math

import numpy as np
import jax
import jax.numpy as jnp
from jax import lax
from jax.experimental import pallas as pl
from jax.experimental.pallas import tpu as pltpu

N_HEADS = 16
HEAD_DIM = 64
N_LRU_BLOCKS = 16
CONV_WIDTH = 4
LRU_C = 8.0
EPS = 1e-6

LANES = 128
MXU_DIM = 256
VMEM_LIMIT_BYTES = 56 * 1024 * 1024
CARRY_ROWS = 8

N_PAIRS = N_HEADS // 2
LOG2E = math.log2(math.e)
N_SPLIT = 3
ATTN_BLOCK = 512
ATTN_Q_BLOCK = 512
TOKEN_BLOCK = 256
PROJ_BLOCK = 256
SAMPLE_KEY_BLOCK = 1024
CUMSUM_BLOCK = 2048

STAT_BASE, STAT_BMAX, STAT_KNORM, STAT_QNORM, N_STATS = 0, 1, 2, 3, 4
STAT_ROWS = 8
NORM_SLACK = 1.01
EXP2_ZERO_BELOW = -150.0
SKIP_MARGIN = 4.0

F32 = jnp.float32
BF16 = jnp.bfloat16
U32 = jnp.uint32


def _log_sigmoid(x):
    return jnp.minimum(x, 0.0) - jnp.log1p(jnp.exp(-jnp.abs(x)))


def _gelu_tanh(x):
    c = math.sqrt(2.0 / math.pi)
    return 0.5 * x * (1.0 + jnp.tanh(c * (x + 0.044715 * (x * x * x))))


def _rmsnorm(x, g):
    return x * lax.rsqrt(jnp.mean(x * x, axis=-1, keepdims=True) + EPS) * g


def _scan_affine_rows(a, b):
    n = a.shape[0]
    row = lax.broadcasted_iota(jnp.int32, a.shape, 0)
    d = 1
    while d < n:
        keep = row >= d
        a_sh = jnp.where(keep, pltpu.roll(a, d, 0), 1.0)
        b_sh = jnp.where(keep, pltpu.roll(b, d, 0), 0.0)
        b = a * b_sh + b
        a = a * a_sh
        d *= 2
    return a, b


def _cumsum_rows(x):
    n = x.shape[0]
    row = lax.broadcasted_iota(jnp.int32, x.shape, 0)
    d = 1
    while d < n:
        x = x + jnp.where(row >= d, pltpu.roll(x, d, 0), 0.0)
        d *= 2
    return x


def _const_spec(shape):
    return pl.BlockSpec(shape, lambda *_: (0,) * len(shape), pipeline_mode=pl.Buffered(1))


def _own_lanes(lane, head):
    return (lane < HEAD_DIM) if head % 2 == 0 else (lane >= HEAD_DIM)


def _aug_lane0(head):
    return HEAD_DIM if head % 2 == 0 else 0


def _zero_from(x):
    u = pltpu.bitcast(x, U32)
    return pltpu.bitcast(lax.shift_right_logical(lax.shift_right_logical(u, U32(16)), U32(16)), F32)


def _proj_lru_kernel(x_ref, conv0_ref, h0_ref, nmix_ref, win_ref, convw_ref, convb_ref, wgate_ref,
                     brg_ref, big_ref, lam_ref, bf_ref, place_ref, sel_ref,
                     q_ref, kb_ref, vb_ref, vt_ref, kf_ref, vf_ref, logf_ref, ylru_ref, convo_ref, hlast_ref,
                     ka_ref, st_ref,
                     xr_buf, h_carry, c_carry, blk_stats, *, reset_first, sub_blocks):
    t = pl.program_id(1)
    tb, d = x_ref.shape[1], x_ref.shape[2]
    a_dim = N_HEADS * HEAD_DIM

    xb = _rmsnorm(x_ref[0], nmix_ref[...]).astype(BF16)

    def proj(lo, width):
        return jnp.dot(xb, win_ref[:, lo:lo + width], preferred_element_type=F32)

    xr = proj(0, d)
    gate = proj(d, d)
    q = proj(2 * d, a_dim)
    k = proj(2 * d + a_dim, a_dim)
    v = proj(2 * d + 2 * a_dim, a_dim)
    f_logit = proj(2 * d + 3 * a_dim, LANES)

    qb = (q * (LOG2E * HEAD_DIM ** -0.5)).astype(BF16)
    kb, vb, vtb = k.astype(BF16), v.astype(BF16), v.T.astype(BF16)
    for hp in range(N_PAIRS):
        ls = slice(hp * LANES, (hp + 1) * LANES)
        q_ref[0, hp] = qb[:, ls]
        kb_ref[0, hp] = kb[:, ls]
        vb_ref[0, hp] = vb[:, ls]
    for h in range(N_HEADS):
        vt_ref[0, h] = vtb[h * HEAD_DIM:(h + 1) * HEAD_DIM, :]
    kf_ref[0] = k
    vf_ref[0] = v
    logf = _log_sigmoid(f_logit + bf_ref[...])
    logf_ref[0] = logf

    first_sub = lax.rem(t, sub_blocks) == 0

    @pl.when(t == 0)
    def _():
        c_carry[...] = jnp.zeros(c_carry.shape, F32)

    c = _cumsum_rows(logf) + c_carry[...]
    c_carry[...] = c[tb - 1:tb, :]
    knorm, qnorm = _max_head_norm(kb, sel_ref), _max_head_norm(qb, sel_ref)

    @pl.when(first_sub)
    def _():
        blk_stats[STAT_BASE:STAT_BASE + 1, :] = c[0:1, :]
        blk_stats[STAT_KNORM:STAT_KNORM + 1, :] = knorm
        blk_stats[STAT_QNORM:STAT_QNORM + 1, :] = qnorm

    @pl.when(jnp.logical_not(first_sub))
    def _():
        blk_stats[STAT_KNORM:STAT_KNORM + 1, :] = jnp.maximum(blk_stats[STAT_KNORM:STAT_KNORM + 1, :], knorm)
        blk_stats[STAT_QNORM:STAT_QNORM + 1, :] = jnp.maximum(blk_stats[STAT_QNORM:STAT_QNORM + 1, :], qnorm)

    base = blk_stats[STAT_BASE:STAT_BASE + 1, :]
    rest = (base - c) * LOG2E
    pieces = []
    for _ in range(N_SPLIT):
        piece = rest.astype(BF16)
        pieces.append(piece)
        rest = rest - piece.astype(F32)
    placed = jnp.dot(jnp.concatenate(pieces, axis=1), place_ref[...], preferred_element_type=F32)
    lane = lax.broadcasted_iota(jnp.int32, (tb, LANES), 1)
    for h in range(N_HEADS):
        ls = slice((h // 2) * LANES, (h // 2 + 1) * LANES)
        ka_ref[0, h] = jnp.where(_own_lanes(lane, h), kb[:, ls], placed[:, ls].astype(BF16))
    zero_row = jnp.zeros((1, LANES), F32)
    stats = {STAT_BASE: base * LOG2E,
             STAT_BMAX: c[tb - 1:tb, :] * -LOG2E,
             STAT_KNORM: blk_stats[STAT_KNORM:STAT_KNORM + 1, :],
             STAT_QNORM: blk_stats[STAT_QNORM:STAT_QNORM + 1, :]}
    st_ref[0, 0] = jnp.concatenate([stats.get(n, zero_row) for n in range(STAT_ROWS)], axis=0)

    lo = CARRY_ROWS - (CONV_WIDTH - 1)

    @pl.when(t == 0)
    def _():
        xr_buf[lo:CARRY_ROWS, :] = conv0_ref[0]
        h_carry[...] = h0_ref[0]

    xr_buf[CARRY_ROWS:CARRY_ROWS + tb, :] = xr
    xc = convb_ref[...] + xr_buf[lo:lo + tb, :] * convw_ref[0:1, :]
    for j in range(1, CONV_WIDTH):
        xc = xc + xr_buf[lo + j:lo + j + tb, :] * convw_ref[j:j + 1, :]
    tail = xr_buf[tb + lo:tb + CARRY_ROWS, :]
    convo_ref[0] = tail
    xr_buf[lo:CARRY_ROWS, :] = tail

    xcb = xc.astype(BF16)
    parts = [jnp.dot(xcb[:, g * MXU_DIM:(g + 1) * MXU_DIM], wgate_ref[g], preferred_element_type=F32)
             for g in range(d // MXU_DIM)]
    r = jax.nn.sigmoid(jnp.concatenate([p[:, :MXU_DIM] for p in parts], axis=1) + brg_ref[...])
    i = jax.nn.sigmoid(jnp.concatenate([p[:, MXU_DIM:] for p in parts], axis=1) + big_ref[...])
    a = jnp.exp(LRU_C * r * _log_sigmoid(lam_ref[...]))
    mult = jnp.sqrt(1.0 - a * a)
    if reset_first:
        row = lax.broadcasted_iota(jnp.int32, mult.shape, 0)
        mult = jnp.where(jnp.logical_and(row == 0, t == 0), 1.0, mult)
    a_cum, b_cum = _scan_affine_rows(a, mult * i * xc)
    h = a_cum * h_carry[...] + b_cum
    h_carry[...] = h[tb - 1:tb, :]
    hlast_ref[0] = h[tb - 1:tb, :]
    ylru_ref[0] = (h * _gelu_tanh(gate)).astype(BF16)


def _proj_lru(x, conv0, h0, w, *, reset_first):
    b, t, d = x.shape
    a_dim = N_HEADS * HEAD_DIM
    tb = min(PROJ_BLOCK, t)
    grid = (b, t // tb)
    attn_blk = min(ATTN_BLOCK, t)
    sub_blocks = attn_blk // tb
    place, sel = _bias_placement(), _head_lane_sum()
    tok = lambda width: pl.BlockSpec((1, tb, width), lambda bi, ti: (bi, ti, 0))
    per_b = lambda rows: pl.BlockSpec((1, rows, d), lambda bi, ti: (bi, 0, 0))
    pair_major = pl.BlockSpec((1, N_PAIRS, tb, LANES), lambda bi, ti: (bi, 0, ti, 0))
    out_shapes = (
        jax.ShapeDtypeStruct((b, N_PAIRS, t, LANES), BF16),
        jax.ShapeDtypeStruct((b, N_PAIRS, t, LANES), BF16),
        jax.ShapeDtypeStruct((b, N_PAIRS, t, LANES), BF16),
        jax.ShapeDtypeStruct((b, N_HEADS, HEAD_DIM, t), BF16),
        jax.ShapeDtypeStruct((b, t, a_dim), F32),
        jax.ShapeDtypeStruct((b, t, a_dim), F32),
        jax.ShapeDtypeStruct((b, t, LANES), F32),
        jax.ShapeDtypeStruct((b, t, d), BF16),
        jax.ShapeDtypeStruct((b, CONV_WIDTH - 1, d), F32),
        jax.ShapeDtypeStruct((b, 1, d), F32),
        jax.ShapeDtypeStruct((b, N_HEADS, t, LANES), BF16),
        jax.ShapeDtypeStruct((b, t // attn_blk, STAT_ROWS, LANES), F32),
    )
    out_specs = (
        pair_major, pair_major, pair_major,
        pl.BlockSpec((1, N_HEADS, HEAD_DIM, tb), lambda bi, ti: (bi, 0, 0, ti)),
        tok(a_dim), tok(a_dim), tok(LANES), tok(d),
        per_b(CONV_WIDTH - 1), per_b(1),
        pl.BlockSpec((1, N_HEADS, tb, LANES), lambda bi, ti: (bi, 0, ti, 0)),
        pl.BlockSpec((1, 1, STAT_ROWS, LANES), lambda bi, ti: (bi, ti // sub_blocks, 0, 0)),
    )
    in_specs = [
        tok(d), per_b(CONV_WIDTH - 1), per_b(1),
        _const_spec(w["norm_mix"].shape), _const_spec(w["w_in1"].shape),
        _const_spec(w["conv_w"].shape), _const_spec(w["conv_b"].shape), _const_spec(w["w_gates"].shape),
        _const_spec(w["b_rg"].shape), _const_spec(w["b_ig"].shape), _const_spec(w["lam"].shape),
        _const_spec(w["b_forget"].shape), _const_spec(place.shape), _const_spec(sel.shape),
    ]
    return pl.pallas_call(
        functools.partial(_proj_lru_kernel, reset_first=reset_first, sub_blocks=sub_blocks),
        grid=grid, in_specs=in_specs, out_specs=out_specs, out_shape=out_shapes,
        scratch_shapes=[pltpu.VMEM((CARRY_ROWS + tb, d), F32), pltpu.VMEM((1, d), F32),
                        pltpu.VMEM((1, LANES), F32), pltpu.VMEM((STAT_ROWS, LANES), F32)],
        compiler_params=pltpu.CompilerParams(
            dimension_semantics=("arbitrary", "arbitrary"), vmem_limit_bytes=VMEM_LIMIT_BYTES),
        name="proj_lru",
    )(x, conv0, h0, w["norm_mix"], w["w_in1"], w["conv_w"], w["conv_b"], w["w_gates"],
      w["b_rg"], w["b_ig"], w["lam"], w["b_forget"], place, sel)


def _bias_placement():
    place = np.zeros((N_SPLIT * LANES, N_PAIRS * LANES), np.float32)
    for h in range(N_HEADS):
        for s in range(N_SPLIT):
            place[s * LANES + h, (h // 2) * LANES + _aug_lane0(h) + s] = 1.0
    return jnp.asarray(place, BF16)


def _head_lane_sum():
    sel = np.zeros((N_PAIRS * LANES, LANES), np.float32)
    for h in range(N_HEADS):
        lo = (h // 2) * LANES + (h % 2) * HEAD_DIM
        sel[lo:lo + HEAD_DIM, h] = 1.0
    return jnp.asarray(sel, BF16)


def _query_ones(rows):
    ones = np.zeros((rows, LANES), np.float32)
    for h in range(2):
        ones[:, _aug_lane0(h):_aug_lane0(h) + N_SPLIT] = 1.0
    return jnp.asarray(ones, BF16)


def _max_head_norm(x, sel_ref):
    sq = jnp.square(x.astype(F32)).astype(BF16)
    norm2 = jnp.dot(sq, sel_ref[...], preferred_element_type=F32)
    return jnp.sqrt(jnp.max(norm2, axis=0, keepdims=True) * NORM_SLACK)


def _cumsum_kernel(lf_ref, off_ref, c_ref, ct_ref, carry):
    @pl.when(pl.program_id(1) == 0)
    def _():
        carry[...] = off_ref[0]

    n = lf_ref.shape[1]
    c = _cumsum_rows(lf_ref[0]) + carry[...]
    carry[...] = c[n - 1:n, :]
    c_ref[0] = c
    ct_ref[0] = c.T


def _cumsum_heads(logf, offset):
    b, t, _ = logf.shape
    tc = min(CUMSUM_BLOCK, t)
    return pl.pallas_call(
        _cumsum_kernel,
        grid=(b, t // tc),
        in_specs=[pl.BlockSpec((1, tc, LANES), lambda bi, ti: (bi, ti, 0)),
                  pl.BlockSpec((1, 1, LANES), lambda bi, ti: (bi, 0, 0))],
        out_specs=(pl.BlockSpec((1, tc, LANES), lambda bi, ti: (bi, ti, 0)),
                   pl.BlockSpec((1, LANES, tc), lambda bi, ti: (bi, 0, ti))),
        out_shape=(jax.ShapeDtypeStruct((b, t, LANES), F32), jax.ShapeDtypeStruct((b, LANES, t), F32)),
        scratch_shapes=[pltpu.VMEM((1, LANES), F32)],
        compiler_params=pltpu.CompilerParams(dimension_semantics=("arbitrary", "arbitrary")),
        name="cumsum_heads",
    )(logf, offset)


def _cumsum_lanes_kernel(lf_ref, c_ref):
    x = lf_ref[0]
    n = x.shape[1]
    lane = lax.broadcasted_iota(jnp.int32, x.shape, 1)
    d = 1
    while d < n:
        x = x + jnp.where(lane >= d, pltpu.roll(x, d, 1), 0.0)
        d *= 2
    c_ref[0] = x


def _cumsum_lanes(logf_t):
    b, h, t = logf_t.shape
    spec = pl.BlockSpec((1, h, t), lambda bi: (bi, 0, 0))
    return pl.pallas_call(
        _cumsum_lanes_kernel, grid=(b,), in_specs=[spec], out_specs=spec,
        out_shape=jax.ShapeDtypeStruct((b, h, t), F32),
        compiler_params=pltpu.CompilerParams(dimension_semantics=("parallel",)),
        name="cumsum_lanes",
    )(logf_t)


def _prompt_attn_kernel(st_ref, order_ref, q_ref, ka_ref, vt_ref, ones_ref, o_ref, qa_ref, m_ref, l_ref, acc_ref):
    bi, j, r = pl.program_id(0), pl.program_id(1), pl.program_id(2)
    nblk = pl.num_programs(2)
    tq, tk = q_ref.shape[2], ka_ref.shape[2]
    ratio = tq // tk
    last = ratio * (j + 1) - 1
    diag = r < ratio
    kb = jnp.where(diag, ratio * j + r, jnp.maximum(last - r, 0))
    stat = lambda blk, which, h: st_ref[((bi * nblk + blk) * N_STATS + which) * N_HEADS + h]

    @pl.when(r == 0)
    def _():
        m_ref[...] = jnp.full(m_ref.shape, -jnp.inf, F32)
        l_ref[...] = jnp.zeros(l_ref.shape, F32)
        acc_ref[...] = jnp.zeros(acc_ref.shape, F32)
        lane = lax.broadcasted_iota(jnp.int32, (tq, LANES), 1)
        for h in range(N_HEADS):
            qa_ref[h] = jnp.where(_own_lanes(lane, h), q_ref[0, h // 2], ones_ref[...])

    def scores(h):
        return lax.dot_general(ka_ref[0, h], qa_ref[h], (((1,), (1,)), ((), ())), preferred_element_type=F32)

    def heads_step(heads, masked):
        if masked:
            first_key = r * tk if ratio > 1 else 0
            allowed = (lax.broadcasted_iota(jnp.int32, (tk, tq), 0) + first_key
                       <= lax.broadcasted_iota(jnp.int32, (tk, tq), 1))
        s_next = scores(heads[0])
        for n, h in enumerate(heads):
            s = s_next
            if n + 1 < len(heads):
                s_next = scores(heads[n + 1])
            if masked:
                s = jnp.where(allowed, s, -jnp.inf)
            base = stat(kb, STAT_BASE, h)
            m_old = m_ref[pl.ds(h, 1), :]
            m_new = jnp.maximum(m_old, jnp.max(s, axis=0, keepdims=True) - base)
            alpha = jnp.exp2(m_old - m_new)
            p = jnp.exp2(s - (m_new + base))
            l_ref[pl.ds(h, 1), :] = alpha * l_ref[pl.ds(h, 1), :] + jnp.sum(p, axis=0, keepdims=True)
            if n + 1 < len(heads):
                alpha = alpha + _zero_from(s_next[tk - 8:tk, tq - LANES:tq])[0:1, 0:1]
            acc_ref[h] = alpha * acc_ref[h] + jnp.dot(vt_ref[0, h], p.astype(BF16), preferred_element_type=F32)
            m_ref[pl.ds(h, 1), :] = m_new

    def is_live(h):
        bmax = stat(kb, STAT_BMAX, h)
        qnorm = functools.reduce(jnp.maximum, [stat(ratio * j + n, STAT_QNORM, h) for n in range(ratio)])
        bound = qnorm * stat(kb, STAT_KNORM, h) + bmax + jnp.abs(bmax) * 2.0 ** -20 + SKIP_MARGIN
        return bound - jnp.min(m_ref[pl.ds(h, 1), :]) > EXP2_ZERO_BELOW

    @pl.when(diag)
    def _():
        heads_step(tuple(range(N_HEADS)), True)

    @pl.when(jnp.logical_and(r >= ratio, r <= last))
    def _():
        heads = [order_ref[bi * N_HEADS + n] for n in range(N_HEADS)]
        live = [is_live(h) for h in heads]
        pair_live = [jnp.logical_or(live[2 * slot], live[2 * slot + 1]) for slot in range(N_PAIRS)]

        def run(lo, n):
            if n == 1:
                @pl.when(pair_live[lo])
                def _():
                    heads_step((heads[2 * lo], heads[2 * lo + 1]), False)
                return
            whole = functools.reduce(jnp.logical_and, pair_live[lo:lo + n])

            @pl.when(whole)
            def _():
                heads_step(tuple(range(N_HEADS)) if n == N_PAIRS else tuple(heads[2 * lo:2 * (lo + n)]), False)

            @pl.when(jnp.logical_not(whole))
            def _():
                run(lo, n // 2)
                run(lo + n // 2, n // 2)

        run(0, N_PAIRS)

    @pl.when(r == last)
    def _():
        inv = 1.0 / l_ref[...]
        for hp in range(N_PAIRS):
            o = jnp.concatenate([acc_ref[2 * hp + e] * inv[2 * hp + e:2 * hp + e + 1, :] for e in range(2)], axis=0)
            o_ref[0, :, hp * LANES:(hp + 1) * LANES] = o.T.astype(BF16)


def _prompt_attention(q, ka, vt, stats):
    b, _, t, _ = q.shape
    tk = min(ATTN_BLOCK, t)
    tq = min(ATTN_Q_BLOCK, t)
    ratio = tq // tk
    nq = t // tq
    ones = _query_ones(tq)
    order = jnp.argsort(stats[:, -1, STAT_BMAX, :N_HEADS], axis=-1).astype(jnp.int32)
    key_block = lambda j, r: jnp.where(r < ratio, ratio * j + r, jnp.maximum(ratio * (j + 1) - 1 - r, 0))
    grid_spec = pltpu.PrefetchScalarGridSpec(
        num_scalar_prefetch=2, grid=(b, nq, t // tk),
        in_specs=[pl.BlockSpec((1, N_PAIRS, tq, LANES), lambda bi, j, r, *_: (bi, 0, j, 0)),
                  pl.BlockSpec((1, N_HEADS, tk, LANES), lambda bi, j, r, *_: (bi, 0, key_block(j, r), 0)),
                  pl.BlockSpec((1, N_HEADS, HEAD_DIM, tk), lambda bi, j, r, *_: (bi, 0, 0, key_block(j, r))),
                  pl.BlockSpec((tq, LANES), lambda bi, j, r, *_: (0, 0))],
        out_specs=pl.BlockSpec((1, tq, N_PAIRS * LANES), lambda bi, j, r, *_: (bi, j, 0)),
        scratch_shapes=[pltpu.VMEM((N_HEADS, tq, LANES), BF16),
                        pltpu.VMEM((N_HEADS, tq), F32), pltpu.VMEM((N_HEADS, tq), F32),
                        pltpu.VMEM((N_HEADS, HEAD_DIM, tq), F32)])
    return pl.pallas_call(
        _prompt_attn_kernel, grid_spec=grid_spec,
        out_shape=jax.ShapeDtypeStruct((b, t, N_PAIRS * LANES), BF16),
        compiler_params=pltpu.CompilerParams(
            dimension_semantics=("parallel", "parallel", "arbitrary"), vmem_limit_bytes=VMEM_LIMIT_BYTES),
        name="prompt_attention",
    )(stats[:, :, :N_STATS, :N_HEADS].reshape(-1), order.reshape(-1), q, ka, vt, ones)


def _sample_attn_kernel(q_ref, pk_ref, pv_ref, kn_ref, vn_ref, cpt_ref, cnt_ref, o_ref,
                        qw_ref, m_ref, l_ref, acc_ref, *, n_past):
    i = pl.program_id(1)
    tq = q_ref.shape[2]
    rows = 2 * tq

    @pl.when(i == 0)
    def _():
        m_ref[...] = jnp.full(m_ref.shape, -jnp.inf, F32)
        l_ref[...] = jnp.zeros(l_ref.shape, F32)
        acc_ref[...] = jnp.zeros(acc_ref.shape, F32)
        lane = lax.broadcasted_iota(jnp.int32, (tq, LANES), 1)
        for hp in range(N_PAIRS):
            pair = q_ref[0, hp]
            for e in range(2):
                qw_ref[hp, e * tq:(e + 1) * tq, :] = jnp.where(_own_lanes(lane, e), pair, jnp.zeros_like(pair))

    def step(k_of, v_of, ckt, masked, dim_major):
        tk = ckt.shape[1]
        key_axis = 1 if dim_major else 0
        contract = lambda a, b, b_axis: lax.dot_general(a, b, (((1,), (b_axis,)), ((), ())),
                                                        preferred_element_type=F32)
        s = jnp.concatenate([contract(qw_ref[hp], k_of(hp), 1 - key_axis) for hp in range(N_PAIRS)],
                            axis=0)
        ck = jnp.concatenate([jnp.broadcast_to(ckt[h:h + 1, :], (tq, tk)) for h in range(N_HEADS)], axis=0)
        s = s - ck * LOG2E
        if masked:
            q_idx = lax.rem(lax.broadcasted_iota(jnp.int32, s.shape, 0), tq)
            s = jnp.where(lax.broadcasted_iota(jnp.int32, s.shape, 1) <= q_idx, s, -jnp.inf)
        m_old = m_ref[...]
        m_new = jnp.maximum(m_old, jnp.max(s, axis=1, keepdims=True))
        alpha = jnp.exp2(m_old - m_new)
        p = jnp.exp2(s - m_new)
        l_ref[...] = alpha * l_ref[...] + jnp.sum(p, axis=1, keepdims=True)
        m_ref[...] = m_new
        pb = p.astype(BF16)
        for hp in range(N_PAIRS):
            rs = slice(hp * rows, (hp + 1) * rows)
            acc_ref[hp] = alpha[rs, :] * acc_ref[hp] + contract(pb[rs, :], v_of(hp), key_axis)

    @pl.when(i < n_past)
    def _():
        step(lambda hp: pk_ref[0, hp].astype(BF16), lambda hp: pv_ref[0, hp].astype(BF16), cpt_ref[0],
             False, True)

    @pl.when(i == n_past)
    def _():
        step(lambda hp: kn_ref[0, hp], lambda hp: vn_ref[0, hp], cnt_ref[0], True, False)
        inv = 1.0 / l_ref[...]
        for hp in range(N_PAIRS):
            o = acc_ref[hp] * inv[hp * rows:(hp + 1) * rows, :]
            for e in range(2):
                lo = hp * LANES + e * HEAD_DIM
                o_ref[0, :, lo:lo + HEAD_DIM] = o[e * tq:(e + 1) * tq, e * HEAD_DIM:(e + 1) * HEAD_DIM].astype(BF16)


def _sample_attention(q, past_kt, past_vt, kn, vn, cpt, cnt):
    b, _, tq, _ = q.shape
    p_len, a_dim = past_kt.shape[3], N_HEADS * HEAD_DIM
    tk = min(SAMPLE_KEY_BLOCK, p_len)
    n_past = p_len // tk
    past = pl.BlockSpec((1, N_PAIRS, LANES, tk), lambda bi, i: (bi, 0, 0, jnp.minimum(i, n_past - 1)))
    whole4 = pl.BlockSpec((1, N_PAIRS, tq, LANES), lambda bi, i: (bi, 0, 0, 0))
    return pl.pallas_call(
        functools.partial(_sample_attn_kernel, n_past=n_past),
        grid=(b, n_past + 1),
        in_specs=[whole4, past, past, whole4, whole4,
                  pl.BlockSpec((1, N_HEADS, tk), lambda bi, i: (bi, 0, jnp.minimum(i, n_past - 1))),
                  pl.BlockSpec((1, N_HEADS, tq), lambda bi, i: (bi, 0, 0))],
        out_specs=pl.BlockSpec((1, tq, a_dim), lambda bi, i: (bi, 0, 0)),
        out_shape=jax.ShapeDtypeStruct((b, tq, a_dim), BF16),
        scratch_shapes=[pltpu.VMEM((N_PAIRS, 2 * tq, LANES), BF16),
                        pltpu.VMEM((N_HEADS * tq, 1), F32), pltpu.VMEM((N_HEADS * tq, 1), F32),
                        pltpu.VMEM((N_PAIRS, 2 * tq, LANES), F32)],
        compiler_params=pltpu.CompilerParams(
            dimension_semantics=("parallel", "arbitrary"), vmem_limit_bytes=VMEM_LIMIT_BYTES),
        name="sample_attention",
    )(q, past_kt, past_vt, kn, vn, cpt, cnt)


def _merge_ffn_kernel(x_ref, ylru_ref, yatt_ref, nmix_ref, wg_ref, bgate_ref, wbl_ref, wba_ref, wout_ref,
                      nffn_ref, wfi_ref, wfo_ref, nfin_ref, y_ref):
    d = x_ref.shape[1]
    d_ff = wfo_ref.shape[0]
    x = x_ref[...]
    xb = _rmsnorm(x, nmix_ref[...]).astype(BF16)
    g = jax.nn.sigmoid(jnp.dot(xb, wg_ref[...], preferred_element_type=F32) + bgate_ref[...])
    mixed = (g[:, :d] * jnp.dot(ylru_ref[...], wbl_ref[...], preferred_element_type=F32)
             + g[:, d:] * jnp.dot(yatt_ref[...], wba_ref[...], preferred_element_type=F32))
    x = x + jnp.dot(mixed.astype(BF16), wout_ref[...], preferred_element_type=F32)
    xb2 = _rmsnorm(x, nffn_ref[...]).astype(BF16)
    hid = jnp.dot(xb2, wfi_ref[...], preferred_element_type=F32)
    gf, up = hid[:, :d_ff], hid[:, d_ff:]
    act = (gf * jax.nn.sigmoid(gf) * up).astype(BF16)
    x = x + jnp.dot(act, wfo_ref[...], preferred_element_type=F32)
    y_ref[...] = _rmsnorm(x, nfin_ref[...])


def _merge_ffn(x, ylru, yatt, w, norm_final):
    n, d = x.shape
    tb = min(TOKEN_BLOCK, n)
    tok = pl.BlockSpec((tb, d), lambda i: (i, 0))
    consts = [w["norm_mix"], w["w_g"], w["b_gate"], w["w_br_lru"], w["w_br_att"], w["w_out"],
              w["norm_ffn"], w["w_ffn_in"], w["w_ffn_out"], norm_final]
    return pl.pallas_call(
        _merge_ffn_kernel,
        grid=(n // tb,),
        in_specs=[tok, tok, tok] + [_const_spec(c.shape) for c in consts],
        out_specs=tok,
        out_shape=jax.ShapeDtypeStruct((n, d), F32),
        compiler_params=pltpu.CompilerParams(
            dimension_semantics=("parallel",), vmem_limit_bytes=VMEM_LIMIT_BYTES),
        name="merge_ffn",
    )(x, ylru, yatt, *consts)


def _prep_layer_weights(norm_mix, w_in, b_forget, b_gate, conv_w, conv_b, w_rg, b_rg, w_ig, b_ig, lam,
                        w_br_lru, w_br_att, w_out, norm_ffn, w_ffn_in, w_ffn_out):
    d = w_in.shape[0]
    a_dim = N_HEADS * HEAD_DIM
    n1 = 2 * d + 3 * a_dim
    row = lambda v: v.reshape(1, -1).astype(F32)
    w_f = jnp.pad(w_in[:, n1 + 2 * d:], ((0, 0), (0, LANES - N_HEADS)))
    per_group = MXU_DIM // (d // N_LRU_BLOCKS)

    def block_diag(wb):
        g = wb.reshape(-1, per_group, wb.shape[1], wb.shape[2])
        eye = jnp.eye(per_group, dtype=wb.dtype)
        return jnp.einsum("gaij,ab->gaibj", g, eye).reshape(g.shape[0], MXU_DIM, MXU_DIM)

    return {
        "norm_mix": row(norm_mix),
        "w_in1": jnp.concatenate([w_in[:, :n1], w_f], axis=1).astype(BF16),
        "w_g": w_in[:, n1:n1 + 2 * d].astype(BF16),
        "b_forget": jnp.pad(row(b_forget), ((0, 0), (0, LANES - N_HEADS))),
        "b_gate": row(b_gate),
        "conv_w": conv_w.astype(F32), "conv_b": row(conv_b),
        "w_gates": jnp.concatenate([block_diag(w_rg), block_diag(w_ig)], axis=2).astype(BF16),
        "b_rg": row(b_rg), "b_ig": row(b_ig), "lam": row(lam),
        "w_br_lru": w_br_lru.astype(BF16), "w_br_att": w_br_att.astype(BF16), "w_out": w_out.astype(BF16),
        "norm_ffn": row(norm_ffn), "w_ffn_in": w_ffn_in.astype(BF16), "w_ffn_out": w_ffn_out.astype(BF16),
    }


def _layer_prompt(x, w, norm_final):
    b, t, d = x.shape
    zeros_conv = jnp.zeros((b, CONV_WIDTH - 1, d), F32)
    zeros_h = jnp.zeros((b, 1, d), F32)
    q, _, _, vt, kf, vf, logf, ylru, conv_o, h_last, ka, stats = _proj_lru(
        x, zeros_conv, zeros_h, w, reset_first=True)
    yatt = _prompt_attention(q, ka, vt, stats)
    y = _merge_ffn(x.reshape(b * t, d), ylru.reshape(b * t, d), yatt.reshape(b * t, d), w, norm_final)
    return y.reshape(b, t, d), kf, vf, logf[..., :N_HEADS], conv_o, h_last[:, 0]


def _layer_sample(x, conv0, h0, past_k, past_v, past_logf, w, norm_final):
    b, t, d = x.shape
    p_len = past_k.shape[1]
    q, kb, vb, _, kf, vf, logf, ylru, conv_o, h_last, _, _ = _proj_lru(
        x, conv0, h0[:, None, :], w, reset_first=False)
    dim_major = lambda kv: jnp.transpose(kv, (0, 2, 3, 1)).reshape(b, N_PAIRS, LANES, p_len)
    ct_past = _cumsum_lanes(jnp.transpose(past_logf.astype(F32), (0, 2, 1)))
    total = jnp.pad(ct_past[:, :, p_len - 1], ((0, 0), (0, LANES - N_HEADS)))[:, None, :]
    _, ct_new = _cumsum_heads(logf, total)
    yatt = _sample_attention(q, dim_major(past_k), dim_major(past_v), kb, vb, ct_past, ct_new)
    y = _merge_ffn(x.reshape(b * t, d), ylru.reshape(b * t, d), yatt.reshape(b * t, d), w, norm_final)
    return y.reshape(b, t, d), kf, vf, logf[..., :N_HEADS], conv_o, h_last[:, 0]


def kernel(x_prompt, x_sample, cache_k, cache_v, cache_logf, state_conv, state_h, norm_mix, w_in, b_forget,
           b_gate, conv_w, conv_b, w_rg, b_rg, w_ig, b_ig, lru_lambda, w_br_lru, w_br_att, w_out, norm_ffn,
           w_ffn_in, w_ffn_out, norm_final):
    assert norm_mix.shape[0] == 1, "single-layer trunk: the final rmsnorm is fused into the layer's last stage"
    heads = lambda kv: kv.reshape(kv.shape[0], kv.shape[1], N_HEADS, HEAD_DIM)[None]
    nfin = norm_final.reshape(1, -1).astype(F32)
    w = _prep_layer_weights(norm_mix[0], w_in[0], b_forget[0], b_gate[0], conv_w[0], conv_b[0], w_rg[0], b_rg[0],
                            w_ig[0], b_ig[0], lru_lambda[0], w_br_lru[0], w_br_att[0], w_out[0], norm_ffn[0],
                            w_ffn_in[0], w_ffn_out[0])
    yp, kp, vp, lfp, cp, hp = _layer_prompt(x_prompt, w, nfin)
    ys, ks, vs, lfs, cs, hs = _layer_sample(x_sample, state_conv[0], state_h[0], cache_k[0], cache_v[0],
                                            cache_logf[0], w, nfin)
    return (yp, ys, heads(kp), heads(vp), lfp[None], cp[None], hp[None],
            heads(ks), heads(vs), lfs[None], cs[None], hs[None])
```
